```python
import numpy as np
import jax
import jax.numpy as jnp
from jax import lax

D_MODEL = 4096
BATCH = 4
SEQ = 4096
DEPTH = 2

N_MOD = 6
RMS_EPS = 1e-6
CONV_WIDTH = 1024
CONV_K = 3
NSA_HEADS = 16
NSA_KV_GROUPS = 4
NSA_HEAD_DIM = 128
CMP_BLOCK = 32
CMP_STRIDE = 16
CMP_HIDDEN = 128
SEL_BLOCK = 64
N_SELECT = 16
WINDOW = 512
SEL_Q_CHUNK = 32
WIN_Q_BLOCK = 128
FORCE_SCORE = 1e9
NEG_INF = -1e30
ROPE_THETA = 500000.0
ROPE_DIMS = NSA_HEAD_DIM // 4
NSA_Q_WIDTH = NSA_HEADS * NSA_HEAD_DIM
NSA_KV_WIDTH = 6 * NSA_KV_GROUPS * NSA_HEAD_DIM
NSA_GATE_WIDTH = 3 * NSA_HEADS
RWKV_HEADS = 16
RWKV_HEAD_DIM = 64
RWKV_WIDTH = RWKV_HEADS * RWKV_HEAD_DIM
RWKV_DECAY_LORA = 64
RWKV_A_LORA = 64
RWKV_GATE_LORA = 160
RWKV_IN_WIDTH = 3 * RWKV_WIDTH + RWKV_DECAY_LORA + RWKV_A_LORA + RWKV_GATE_LORA
RWKV_LN_EPS = 64e-5
N_BRANCHES = 3
MIX_WIDTH = CONV_WIDTH + NSA_Q_WIDTH + RWKV_WIDTH
IN_WIDTH = 3 * CONV_WIDTH + NSA_Q_WIDTH + NSA_KV_WIDTH + NSA_GATE_WIDTH + RWKV_IN_WIDTH + N_BRANCHES * D_MODEL
N_EXPERTS = 32
TOP_K = 4
EXPERT_FF = 512
SWIGLU_ALPHA = 1.702
SWIGLU_LIMIT = 7.0
MOE_BLOCK = 128

kernel_name = 'hybrid_conv_nsa_rwkv7_moe_adaln'


def rms_norm(x, g):
    xf = x.astype(jnp.float32)
    y = xf * lax.rsqrt(jnp.mean(xf * xf, axis=-1, keepdims=True) + RMS_EPS)
    return (y * g.astype(jnp.float32)).astype(x.dtype)


def rope_tables(seq):
    inv = ROPE_THETA ** (-jnp.arange(0, ROPE_DIMS, 2, dtype=jnp.float32) / ROPE_DIMS)
    ang = jnp.arange(seq, dtype=jnp.float32)[:, None] * inv[None, :]
    return jnp.cos(ang), jnp.sin(ang)


def apply_rope(x, cos, sin):
    half = ROPE_DIMS // 2
    c = cos[None, :, None, :].astype(x.dtype)
    s = sin[None, :, None, :].astype(x.dtype)
    x1 = x[..., :half]
    x2 = x[..., half:ROPE_DIMS]
    return jnp.concatenate([x1 * c - x2 * s, x2 * c + x1 * s, x[..., ROPE_DIMS:]], axis=-1)


def causal_dwconv(u, w):
    return lax.conv_general_dilated(
        u, w[:, None, :].astype(u.dtype), window_strides=(1,),
        padding=[(CONV_K - 1, 0)], dimension_numbers=('NWC', 'WIO', 'NWC'),
        feature_group_count=u.shape[-1])


def compress_blocks(t, pe, w1, w2):
    Bb, S, G, Dh = t.shape
    r = CMP_BLOCK // CMP_STRIDE
    n_sub = S // CMP_STRIDE
    nc = n_sub - r + 1
    sub = t.reshape(Bb, n_sub, CMP_STRIDE, G, Dh)
    blocks = jnp.concatenate([sub[:, i:i + nc] for i in range(r)], axis=2)
    blocks = blocks + pe[None, None, :, None, :]
    flat = blocks.transpose(0, 1, 3, 2, 4).reshape(Bb, nc, G, CMP_BLOCK * Dh)
    return jax.nn.gelu(flat @ w1) @ w2


def block_map(nc, nb):
    i = np.arange(nc)[:, None] * CMP_STRIDE
    j = np.arange(nb)[None, :] * SEL_BLOCK
    inter = np.clip(np.minimum(i + CMP_BLOCK, j + SEL_BLOCK) - np.maximum(i, j), 0, None)
    return jnp.asarray(inter.astype(np.float32) / np.float32(CMP_BLOCK))


def nsa_attention(q, kc, vc, ks, vs, kw, vw, gates, cmp_pe, cmp_w1, cmp_w2):
    Bb, S, H, Dh = q.shape
    G = kc.shape[2]
    hpg = H // G
    f32 = jnp.float32
    scale = Dh ** -0.5
    qg = q.reshape(Bb, S, G, hpg, Dh)
    t = jnp.arange(S)

    kcmp = compress_blocks(kc, cmp_pe[0], cmp_w1[0], cmp_w2[0])
    vcmp = compress_blocks(vc, cmp_pe[1], cmp_w1[1], cmp_w2[1])
    nc = kcmp.shape[1]
    s_c = jnp.einsum('bsghd,bngd->bghsn', qg, kcmp, preferred_element_type=f32) * scale
    valid_c = (jnp.arange(nc) * CMP_STRIDE + CMP_BLOCK - 1)[None, :] <= t[:, None]
    p_c = jax.nn.softmax(jnp.where(valid_c, s_c, NEG_INF), axis=-1)
    p_c = jnp.where(valid_c, p_c, 0.0)
    o_c = jnp.einsum('bghsn,bngd->bsghd', p_c.astype(vcmp.dtype), vcmp)

    nb = S // SEL_BLOCK
    imp = jnp.einsum('bghsn,nj->bgsj', p_c, block_map(nc, nb))
    jb = jnp.arange(nb)[None, :]
    cur = (t // SEL_BLOCK)[:, None]
    forced = (jb == 0) | (jb == cur) | (jb == cur - 1)
    imp = jnp.where(forced, FORCE_SCORE, imp)
    imp = jnp.where(jb <= cur, imp, NEG_INF)
    n_sel = min(N_SELECT, nb)
    _, sel_idx = lax.top_k(imp, n_sel)

    kb = ks.reshape(Bb, nb, SEL_BLOCK, G, Dh).transpose(0, 3, 1, 2, 4)
    vb = vs.reshape(Bb, nb, SEL_BLOCK, G, Dh).transpose(0, 3, 1, 2, 4)
    nq = S // SEL_Q_CHUNK
    q_ch = qg.reshape(Bb, nq, SEL_Q_CHUNK, G, hpg, Dh).transpose(1, 0, 2, 3, 4, 5)
    idx_ch = sel_idx.reshape(Bb, G, nq, SEL_Q_CHUNK, n_sel).transpose(2, 0, 1, 3, 4)
    t_ch = t.reshape(nq, SEL_Q_CHUNK)
    bi = jnp.arange(Bb)[:, None, None, None]
    gi = jnp.arange(G)[None, :, None, None]

    def sel_chunk(args):
        qc, ic, tc = args
        k_sel = kb[bi, gi, ic]
        v_sel = vb[bi, gi, ic]
        s = jnp.einsum('bqghd,bgqnld->bghqnl', qc, k_sel, preferred_element_type=f32) * scale
        kpos = ic[..., None] * SEL_BLOCK + jnp.arange(SEL_BLOCK)
        mask = kpos <= tc[None, None, :, None, None]
        s = jnp.where(mask[:, :, None], s, NEG_INF)
        p = jax.nn.softmax(s.reshape(Bb, G, hpg, SEL_Q_CHUNK, n_sel * SEL_BLOCK), axis=-1)
        p = p.reshape(Bb, G, hpg, SEL_Q_CHUNK, n_sel, SEL_BLOCK)
        return jnp.einsum('bghqnl,bgqnld->bqghd', p.astype(v_sel.dtype), v_sel)

    o_s = lax.map(sel_chunk, (q_ch, idx_ch, t_ch))
    o_s = o_s.transpose(1, 0, 2, 3, 4, 5).reshape(Bb, S, G, hpg, Dh)

    kwp = jnp.pad(kw, ((0, 0), (WINDOW, 0), (0, 0), (0, 0)))
    vwp = jnp.pad(vw, ((0, 0), (WINDOW, 0), (0, 0), (0, 0)))
    nqb = S // WIN_Q_BLOCK
    span = WINDOW + WIN_Q_BLOCK
    q_blk = qg.reshape(Bb, nqb, WIN_Q_BLOCK, G, hpg, Dh).transpose(1, 0, 2, 3, 4, 5)

    def win_block(args):
        qb, b = args
        start = b * WIN_Q_BLOCK
        kblk = lax.dynamic_slice_in_dim(kwp, start, span, axis=1)
        vblk = lax.dynamic_slice_in_dim(vwp, start, span, axis=1)
        s = jnp.einsum('bqghd,bkgd->bghqk', qb, kblk, preferred_element_type=f32) * scale
        qpos = start + jnp.arange(WIN_Q_BLOCK)
        kpos = start - WINDOW + jnp.arange(span)
        diff = qpos[:, None] - kpos[None, :]
        mask = (diff >= 0) & (diff < WINDOW) & (kpos[None, :] >= 0)
        p = jax.nn.softmax(jnp.where(mask, s, NEG_INF), axis=-1)
        return jnp.einsum('bghqk,bkgd->bqghd', p.astype(vblk.dtype), vblk)

    o_w = lax.map(win_block, (q_blk, jnp.arange(nqb)))
    o_w = o_w.transpose(1, 0, 2, 3, 4, 5).reshape(Bb, S, G, hpg, Dh)

    g = gates.reshape(Bb, S, G, hpg, 3)
    o = g[..., 0:1] * o_c + g[..., 1:2] * o_s + g[..., 2:3] * o_w
    return o.reshape(Bb, S, H * Dh)


def rwkv7_time_mix(u, mu, w0, wb, a0, ab, gb, kk_scale, ka, rk, ln_w, ln_b):
    Bb, S, _ = u.shape
    H, N, W = RWKV_HEADS, RWKV_HEAD_DIM, RWKV_WIDTH
    f32 = jnp.float32
    u_prev = jnp.pad(u, ((0, 0), (1, 0), (0, 0)))[:, :-1]
    um = u + (u_prev - u) * mu
    r, k, v, lw, la, lg = jnp.split(
        um, [W, 2 * W, 3 * W, 3 * W + RWKV_DECAY_LORA, 3 * W + RWKV_DECAY_LORA + RWKV_A_LORA], axis=-1)
    w = -jax.nn.softplus(-(w0 + jnp.tanh(lw) @ wb).astype(f32)) - 0.5
    decay = jnp.exp(-jnp.exp(w))
    a = jax.nn.sigmoid((a0 + la @ ab).astype(f32))
    g = jax.nn.sigmoid(lg) @ gb

    def heads(z):
        return z.astype(f32).reshape(Bb, S, H, N)

    r, k, v, decay, a = heads(r), heads(k), heads(v), heads(decay), heads(a)
    kk = k * kk_scale.astype(f32).reshape(H, N)
    kk = kk / jnp.maximum(jnp.sqrt(jnp.sum(kk * kk, axis=-1, keepdims=True)), 1e-12)
    k = k * (1.0 + (a - 1.0) * ka.astype(f32).reshape(H, N))

    def step(state, inp):
        r_t, w_t, k_t, v_t, kk_t, a_t = inp
        sa = jnp.einsum('bhvk,bhk->bhv', state, -kk_t)
        state = (state * w_t[:, :, None, :]
                 + sa[..., None] * (kk_t * a_t)[:, :, None, :]
                 + v_t[..., None] * k_t[:, :, None, :])
        return state, jnp.einsum('bhvk,bhk->bhv', state, r_t)

    xs = tuple(jnp.moveaxis(z, 1, 0) for z in (r, decay, k, v, kk, a))
    _, o = lax.scan(step, jnp.zeros((Bb, H, N, N), f32), xs)
    o = jnp.moveaxis(o, 0, 1)
    mean = jnp.mean(o, axis=-1, keepdims=True)
    var = jnp.mean(jnp.square(o - mean), axis=-1, keepdims=True)
    o = (o - mean) * lax.rsqrt(var + RWKV_LN_EPS)
    o = o * ln_w.astype(f32).reshape(H, N) + ln_b.astype(f32).reshape(H, N)
    o = o + jnp.sum(r * k * rk.astype(f32), axis=-1, keepdims=True) * v
    return (o.reshape(Bb, S, W) * g.astype(f32)).astype(u.dtype)


def hybrid_mixer(h, w_in, conv_w, cmp_pe, cmp_w1, cmp_w2, rwkv_mu, rwkv_w0, rwkv_wb, rwkv_a0,
                 rwkv_ab, rwkv_gb, rwkv_kk, rwkv_ka, rwkv_rk, rwkv_ln_w, rwkv_ln_b,
                 w_branch, w_out, cos, sin):
    Bb, S, D = h.shape
    z = h @ w_in
    offs = [int(o) for o in np.cumsum(
        [3 * CONV_WIDTH, NSA_Q_WIDTH, NSA_KV_WIDTH, NSA_GATE_WIDTH, RWKV_IN_WIDTH])]
    z_conv, z_q, z_kv, z_g, z_rwkv, z_merge = jnp.split(z, offs, axis=-1)

    cb, cc, ch = jnp.split(z_conv, 3, axis=-1)
    y_a = cb * causal_dwconv(cc * ch, conv_w)

    q = apply_rope(z_q.reshape(Bb, S, NSA_HEADS, NSA_HEAD_DIM), cos, sin)
    kv = z_kv.reshape(Bb, S, 6, NSA_KV_GROUPS, NSA_HEAD_DIM)
    kc = apply_rope(kv[:, :, 0], cos, sin)
    ks = apply_rope(kv[:, :, 2], cos, sin)
    kw = apply_rope(kv[:, :, 4], cos, sin)
    gates = jax.nn.sigmoid(z_g.reshape(Bb, S, NSA_HEADS, 3))
    y_b = nsa_attention(q, kc, kv[:, :, 1], ks, kv[:, :, 3], kw, kv[:, :, 5], gates,
                        cmp_pe, cmp_w1, cmp_w2)

    y_c = rwkv7_time_mix(z_rwkv, rwkv_mu, rwkv_w0, rwkv_wb, rwkv_a0, rwkv_ab, rwkv_gb,
                         rwkv_kk, rwkv_ka, rwkv_rk, rwkv_ln_w, rwkv_ln_b)

    gm = jax.nn.sigmoid(z_merge.reshape(Bb, S, N_BRANCHES, D))
    pa, pb, pc = jnp.split(w_branch, [CONV_WIDTH, CONV_WIDTH + NSA_Q_WIDTH], axis=0)
    m = gm[:, :, 0] * (y_a @ pa) + gm[:, :, 1] * (y_b @ pb) + gm[:, :, 2] * (y_c @ pc)
    return m @ w_out


def clamped_swiglu(g, u):
    g = jnp.minimum(g, SWIGLU_LIMIT)
    u = jnp.clip(u, -SWIGLU_LIMIT, SWIGLU_LIMIT)
    return g * jax.nn.sigmoid(SWIGLU_ALPHA * g) * (u + 1.0)


def moe_ffn(h, router_w, router_b, wg, bg, wu, bu, wd, bd):
    Bb, S, D = h.shape
    T = Bb * S
    A = T * TOP_K
    f32 = jnp.float32
    ht = h.reshape(T, D)
    logits = jnp.dot(ht, router_w, preferred_element_type=f32) + router_b.astype(f32)
    top_val, top_idx = lax.top_k(logits, TOP_K)
    top_w = jax.nn.softmax(top_val, axis=-1)
    flat_e = top_idx.reshape(A)
    flat_tok = jnp.arange(A, dtype=jnp.int32) // TOP_K
    flat_w = top_w.reshape(A)
    order = jnp.argsort(flat_e)
    sorted_e = flat_e[order]
    counts = jnp.bincount(flat_e, length=N_EXPERTS)
    starts = jnp.cumsum(counts) - counts
    padded = (counts + MOE_BLOCK - 1) // MOE_BLOCK * MOE_BLOCK
    pends = jnp.cumsum(padded)
    pstarts = pends - padded
    dest = pstarts[sorted_e] + jnp.arange(A, dtype=jnp.int32) - starts[sorted_e]
    n_blk = -(-(A + N_EXPERTS * (MOE_BLOCK - 1)) // MOE_BLOCK)
    P = n_blk * MOE_BLOCK
    row_tok = jnp.full((P,), T, jnp.int32).at[dest].set(flat_tok[order])
    row_w = jnp.zeros((P,), f32).at[dest].set(flat_w[order])
    blk_e = jnp.minimum(jnp.searchsorted(pends, jnp.arange(n_blk) * MOE_BLOCK, side='right'),
                        N_EXPERTS - 1)
    ht_pad = jnp.concatenate([ht, jnp.zeros((1, D), ht.dtype)], axis=0)

    def body(acc, blk):
        toks, wts, e = blk
        xb = ht_pad[toks]
        y = clamped_swiglu(xb @ wg[e] + bg[e], xb @ wu[e] + bu[e]) @ wd[e] + bd[e]
        return acc.at[toks].add(y.astype(f32) * wts[:, None]), None

    acc, _ = lax.scan(body, jnp.zeros((T + 1, D), f32),
                      (row_tok.reshape(n_blk, MOE_BLOCK), row_w.reshape(n_blk, MOE_BLOCK), blk_e))
    return acc[:T].reshape(Bb, S, D).astype(h.dtype)


def setup_inputs(seed: int = 0) -> dict:
    key = jax.random.key(seed)
    ks = jax.random.split(key, 40)
    f32 = jnp.float32
    L, D, E, F = DEPTH, D_MODEL, N_EXPERTS, EXPERT_FF

    def nrm(i, shape, scale):
        return jax.random.normal(ks[i], shape, f32) * scale

    def uni(i, shape, lo, hi):
        return jax.random.uniform(ks[i], shape, f32, lo, hi)

    return {
        'x': nrm(0, (BATCH, SEQ, D), 1.0),
        'c': nrm(1, (BATCH, D), 1.0),
        'ada_w': nrm(2, (D, N_MOD * D), 0.25 * D ** -0.5),
        'ada_b': nrm(3, (N_MOD * D,), 0.02),
        'ada_table': nrm(4, (L, N_MOD, D), 0.1),
        'norm_g': 1.0 + nrm(5, (L, 2, D), 0.05),
        'final_g': 1.0 + nrm(6, (D,), 0.05),
        'w_in': nrm(7, (L, D, IN_WIDTH), D ** -0.5),
        'conv_w': nrm(8, (L, CONV_K, CONV_WIDTH), CONV_K ** -0.5),
        'cmp_pe': nrm(9, (L, 2, CMP_BLOCK, NSA_HEAD_DIM), 0.1),
        'cmp_w1': nrm(10, (L, 2, CMP_BLOCK * NSA_HEAD_DIM, CMP_HIDDEN), (CMP_BLOCK * NSA_HEAD_DIM) ** -0.5),
        'cmp_w2': nrm(11, (L, 2, CMP_HIDDEN, NSA_HEAD_DIM), CMP_HIDDEN ** -0.5),
        'rwkv_mu': uni(12, (L, RWKV_IN_WIDTH), 0.0, 1.0),
        'rwkv_w0': uni(13, (L, RWKV_WIDTH), -6.0, -1.0),
        'rwkv_wb': nrm(14, (L, RWKV_DECAY_LORA, RWKV_WIDTH), 0.5 * RWKV_DECAY_LORA ** -0.5),
        'rwkv_a0': nrm(15, (L, RWKV_WIDTH), 0.5),
        'rwkv_ab': nrm(16, (L, RWKV_A_LORA, RWKV_WIDTH), 0.5 * RWKV_A_LORA ** -0.5),
        'rwkv_gb': nrm(17, (L, RWKV_GATE_LORA, RWKV_WIDTH), RWKV_GATE_LORA ** -0.5),
        'rwkv_kk': 0.85 + nrm(18, (L, RWKV_WIDTH), 0.05),
        'rwkv_ka': 1.0 + nrm(19, (L, RWKV_WIDTH), 0.05),
        'rwkv_rk': nrm(20, (L, RWKV_HEADS, RWKV_HEAD_DIM), 0.1),
        'rwkv_ln_w': 1.0 + nrm(21, (L, RWKV_WIDTH), 0.05),
        'rwkv_ln_b': nrm(22, (L, RWKV_WIDTH), 0.02),
        'w_branch': nrm(23, (L, MIX_WIDTH, D), CONV_WIDTH ** -0.5),
        'w_out': nrm(24, (L, D, D), D ** -0.5),
        'router_w': nrm(25, (L, D, E), D ** -0.5),
        'router_b': nrm(26, (L, E), 0.01),
        'exp_wg': nrm(27, (L, E, D, F), D ** -0.5),
        'exp_bg': nrm(28, (L, E, F), 0.01),
        'exp_wu': nrm(29, (L, E, D, F), D ** -0.5),
        'exp_bu': nrm(30, (L, E, F), 0.01),
        'exp_wd': nrm(31, (L, E, F, D), F ** -0.5),
        'exp_bd': nrm(32, (L, E, D), 0.01),
    }


def reference(x, c, ada_w, ada_b, ada_table, norm_g, final_g, w_in, conv_w, cmp_pe, cmp_w1,
              cmp_w2, rwkv_mu, rwkv_w0, rwkv_wb, rwkv_a0, rwkv_ab, rwkv_gb, rwkv_kk, rwkv_ka,
              rwkv_rk, rwkv_ln_w, rwkv_ln_b, w_branch, w_out, router_w, router_b, exp_wg,
              exp_bg, exp_wu, exp_bu, exp_wd, exp_bd):
    Bb, S, D = x.shape
    cos, sin = rope_tables(S)
    mod_all = (jax.nn.silu(c) @ ada_w + ada_b).reshape(Bb, N_MOD, D)
    for l in range(DEPTH):
        mod = mod_all + ada_table[l]
        shift1 = mod[:, 0, None, :]
        scale1 = mod[:, 1, None, :]
        gate1 = mod[:, 2, None, :]
        shift2 = mod[:, 3, None, :]
        scale2 = mod[:, 4, None, :]
        gate2 = mod[:, 5, None, :]
        h = rms_norm(x, norm_g[l, 0]) * (1.0 + scale1) + shift1
        x = x + gate1 * hybrid_mixer(
            h, w_in[l], conv_w[l], cmp_pe[l], cmp_w1[l], cmp_w2[l], rwkv_mu[l], rwkv_w0[l],
            rwkv_wb[l], rwkv_a0[l], rwkv_ab[l], rwkv_gb[l], rwkv_kk[l], rwkv_ka[l], rwkv_rk[l],
            rwkv_ln_w[l], rwkv_ln_b[l], w_branch[l], w_out[l], cos, sin)
        h = rms_norm(x, norm_g[l, 1]) * (1.0 + scale2) + shift2
        x = x + gate2 * moe_ffn(h, router_w[l], router_b[l], exp_wg[l], exp_bg[l], exp_wu[l],
                                exp_bu[l], exp_wd[l], exp_bd[l])
    return rms_norm(x, final_g)
```

```python
import functools

import numpy as np
import jax
import jax.numpy as jnp
from jax import lax
from jax.experimental import pallas as pl
from jax.experimental.pallas import tpu as pltpu

F32 = jnp.float32
BF16 = jnp.bfloat16
I32 = jnp.int32
HI = lax.Precision.HIGHEST

N_MOD = 6
RMS_EPS = 1e-6
CONV_WIDTH = 1024
CONV_K = 3
NSA_HEADS = 16
NSA_KV_GROUPS = 4
NSA_HPG = NSA_HEADS // NSA_KV_GROUPS
NSA_HEAD_DIM = 128
CMP_BLOCK = 32
CMP_STRIDE = 16
SEL_BLOCK = 64
N_SELECT = 16
WINDOW = 512
FORCE_SCORE = 1e9
NEG_INF = -1e30
ROPE_THETA = 500000.0
ROPE_DIMS = NSA_HEAD_DIM // 4
NSA_Q_WIDTH = NSA_HEADS * NSA_HEAD_DIM
NSA_G_WIDTH = NSA_KV_GROUPS * NSA_HEAD_DIM
NSA_GATE_WIDTH = 3 * NSA_HEADS
RWKV_HEADS = 16
RWKV_HEAD_DIM = 64
RWKV_WIDTH = RWKV_HEADS * RWKV_HEAD_DIM
RWKV_DECAY_LORA = 64
RWKV_A_LORA = 64
RWKV_GATE_LORA = 160
RWKV_IN_WIDTH = 3 * RWKV_WIDTH + RWKV_DECAY_LORA + RWKV_A_LORA + RWKV_GATE_LORA
RWKV_LN_EPS = 64e-5
RWKV_CHUNK = 64
TOP_K = 4
SWIGLU_ALPHA = 1.702
SWIGLU_LIMIT = 7.0

LANES = 128
SUBLANES = 8
VMEM_BYTES_V7X = 64 * 1024 * 1024
VMEM_LIMIT = VMEM_BYTES_V7X - 12 * 1024 * 1024

RWKV_PAD_WIDTH = -(-RWKV_IN_WIDTH // 512) * 512
RWKV_TAIL = 3 * RWKV_WIDTH


def _params(*sem):
    return pltpu.CompilerParams(dimension_semantics=sem, vmem_limit_bytes=VMEM_LIMIT)


def _pick(n, pref):
    t = min(n, pref)
    while n % t:
        t //= 2
    return t


def _dot(a, b, **kw):
    return jnp.dot(a, b, preferred_element_type=F32, **kw)


def _dot_nt(a, b, **kw):
    return lax.dot_general(a, b, (((1,), (1,)), ((), ())), preferred_element_type=F32, **kw)


def _dot_tn(a, b, **kw):
    return lax.dot_general(a, b, (((0,), (0,)), ((), ())), preferred_element_type=F32, **kw)


def _ada_kernel(c_ref, w_ref, b_ref, o_ref):
    c = c_ref[...]
    sc = c * jax.nn.sigmoid(c)
    o_ref[...] = _dot(sc, w_ref[...], precision=HI) + b_ref[...]


def ada_modulation(c, ada_w, ada_b):
    B, D = c.shape
    N = ada_w.shape[1]
    rows = -(-B // SUBLANES) * SUBLANES
    cp = jnp.zeros((rows, D), F32).at[:B].set(c)
    tn = _pick(N, 512)
    out = pl.pallas_call(
        _ada_kernel,
        grid=(N // tn,),
        in_specs=[pl.BlockSpec((rows, D), lambda j: (0, 0)),
                  pl.BlockSpec((D, tn), lambda j: (0, j)),
                  pl.BlockSpec((1, tn), lambda j: (0, j))],
        out_specs=pl.BlockSpec((rows, tn), lambda j: (0, j)),
        out_shape=jax.ShapeDtypeStruct((rows, N), F32),
        compiler_params=_params("parallel"),
        name="ada_mod",
    )(cp, ada_w, ada_b.reshape(1, N))
    return out[:B]


def _norm_mod(x, g, sc, sh):
    ms = jnp.mean(x * x, axis=-1, keepdims=True)
    y = x * lax.rsqrt(ms + RMS_EPS) * g
    return y * (1.0 + sc) + sh


def _norm_mod_kernel(x_ref, g_ref, sc_ref, sh_ref, o_ref):
    h = _norm_mod(x_ref[...], g_ref[...], sc_ref[0], sh_ref[0])
    o_ref[...] = h.astype(o_ref.dtype)


def norm_modulate(x, g, scale, shift, S, out_dtype):
    T, D = x.shape
    B = scale.shape[0]
    tm = _pick(S, 256)
    spt = S // tm
    return pl.pallas_call(
        _norm_mod_kernel,
        grid=(T // tm,),
        in_specs=[pl.BlockSpec((tm, D), lambda i: (i, 0)),
                  pl.BlockSpec((1, D), lambda i: (0, 0)),
                  pl.BlockSpec((1, 1, D), lambda i: (i // spt, 0, 0)),
                  pl.BlockSpec((1, 1, D), lambda i: (i // spt, 0, 0))],
        out_specs=pl.BlockSpec((tm, D), lambda i: (i, 0)),
        out_shape=jax.ShapeDtypeStruct((T, D), out_dtype),
        compiler_params=_params("parallel"),
        name="norm_mod",
    )(x, g.reshape(1, D), scale.reshape(B, 1, D), shift.reshape(B, 1, D))


def _norm_router_kernel(x_ref, g_ref, sc_ref, sh_ref, rw_ref, rb_ref, h_ref, idx_ref, wt_ref):
    h = _norm_mod(x_ref[...], g_ref[...], sc_ref[0], sh_ref[0])
    h_ref[...] = h
    logits = _dot(h, rw_ref[...], precision=HI) + rb_ref[...]
    lane = lax.broadcasted_iota(I32, logits.shape, 1)
    lanef = lane.astype(F32)
    idx_out = jnp.zeros(logits.shape, I32)
    val_out = jnp.zeros(logits.shape, F32)
    v0 = None
    for k in range(TOP_K):
        mx = jnp.max(logits, axis=1, keepdims=True)
        first = jnp.min(jnp.where(logits == mx, lanef, float(LANES)), axis=1, keepdims=True)
        hit = lanef == first
        if k == 0:
            v0 = mx
        idx_out = jnp.where(lane == k, first.astype(I32), idx_out)
        val_out = jnp.where(lane == k, jnp.exp(mx - v0), val_out)
        logits = jnp.where(hit, -3e38, logits)
    idx_ref[...] = idx_out
    wt_ref[...] = val_out / jnp.sum(val_out, axis=1, keepdims=True)


def norm_router(x, g, scale, shift, router_w, router_b, S):
    T, D = x.shape
    B = scale.shape[0]
    E = router_w.shape[1]
    assert E <= LANES
    rw = jnp.zeros((D, LANES), F32).at[:, :E].set(router_w)
    rb = jnp.full((1, LANES), NEG_INF, F32).at[0, :E].set(router_b)
    tm = _pick(S, 256)
    spt = S // tm
    return pl.pallas_call(
        _norm_router_kernel,
        grid=(T // tm,),
        in_specs=[pl.BlockSpec((tm, D), lambda i: (i, 0)),
                  pl.BlockSpec((1, D), lambda i: (0, 0)),
                  pl.BlockSpec((1, 1, D), lambda i: (i // spt, 0, 0)),
                  pl.BlockSpec((1, 1, D), lambda i: (i // spt, 0, 0)),
                  pl.BlockSpec((D, LANES), lambda i: (0, 0)),
                  pl.BlockSpec((1, LANES), lambda i: (0, 0))],
        out_specs=[pl.BlockSpec((tm, D), lambda i: (i, 0)),
                   pl.BlockSpec((tm, LANES), lambda i: (i, 0)),
                   pl.BlockSpec((tm, LANES), lambda i: (i, 0))],
        out_shape=[jax.ShapeDtypeStruct((T, D), F32),
                   jax.ShapeDtypeStruct((T, LANES), I32),
                   jax.ShapeDtypeStruct((T, LANES), F32)],
        compiler_params=_params("parallel"),
        name="norm_router",
    )(x, g.reshape(1, D), scale.reshape(B, 1, D), shift.reshape(B, 1, D), rw, rb)


def _final_norm_kernel(x_ref, g_ref, o_ref):
    x = x_ref[...]
    ms = jnp.mean(x * x, axis=-1, keepdims=True)
    o_ref[...] = x * lax.rsqrt(ms + RMS_EPS) * g_ref[...]


def final_norm(x, g):
    T, D = x.shape
    tm = _pick(T, 256)
    return pl.pallas_call(
        _final_norm_kernel,
        grid=(T // tm,),
        in_specs=[pl.BlockSpec((tm, D), lambda i: (i, 0)), pl.BlockSpec((1, D), lambda i: (0, 0))],
        out_specs=pl.BlockSpec((tm, D), lambda i: (i, 0)),
        out_shape=jax.ShapeDtypeStruct((T, D), F32),
        compiler_params=_params("parallel"),
        name="final_norm",
    )(x, g.reshape(1, D))


def _rope_lanes(x, cf, sa, sb):
    half = ROPE_DIMS // 2
    return x * cf + pltpu.roll(x, LANES - half, 1) * sa + pltpu.roll(x, half, 1) * sb


def _mm_plain_kernel(a_ref, b_ref, o_ref):
    o_ref[...] = _dot(a_ref[...], b_ref[...]).astype(o_ref.dtype)


def _mm_sigmoid_kernel(a_ref, b_ref, o_ref):
    o_ref[...] = jax.nn.sigmoid(_dot(a_ref[...], b_ref[...])).astype(o_ref.dtype)


def _mm_rope_kernel(a_ref, b_ref, cf_ref, sa_ref, sb_ref, o_ref):
    acc = _dot(a_ref[...], b_ref[...])
    cf, sa, sb = cf_ref[...], sa_ref[...], sb_ref[...]
    for n in range(acc.shape[1] // LANES):
        sl = slice(n * LANES, (n + 1) * LANES)
        o_ref[:, sl] = _rope_lanes(acc[:, sl], cf, sa, sb).astype(o_ref.dtype)


def _mm_residual_kernel(a_ref, b_ref, x_ref, gate_ref, o_ref):
    o_ref[...] = x_ref[...] + gate_ref[0] * _dot(a_ref[...], b_ref[...])


def _mm_call(kernel, a, b, out_dtype, extra=(), extra_specs=(), tm_pref=1024, tn_pref=512, name="mm"):
    M, K = a.shape
    N = b.shape[1]
    tm, tn = _pick(M, tm_pref), _pick(N, tn_pref)
    return pl.pallas_call(
        kernel,
        grid=(M // tm, N // tn),
        in_specs=[pl.BlockSpec((tm, K), lambda i, j: (i, 0)),
                  pl.BlockSpec((K, tn), lambda i, j: (0, j))] + [s(tm, tn) for s in extra_specs],
        out_specs=pl.BlockSpec((tm, tn), lambda i, j: (i, j)),
        out_shape=jax.ShapeDtypeStruct((M, N), out_dtype),
        compiler_params=_params("parallel", "arbitrary"),
        name=name,
    )(a, b, *extra)


def rope_lane_tables(S):
    half = ROPE_DIMS // 2
    inv = ROPE_THETA ** (-jnp.arange(0, ROPE_DIMS, 2, dtype=F32) / ROPE_DIMS)
    ang = jnp.arange(S, dtype=F32)[:, None] * inv[None, :]
    cos, sin = jnp.cos(ang), jnp.sin(ang)
    zeros = jnp.zeros((S, LANES - ROPE_DIMS), F32)
    zh = jnp.zeros((S, half), F32)
    cf = jnp.concatenate([cos, cos, zeros + 1.0], axis=1)
    sa = jnp.concatenate([-sin, zh, zeros], axis=1)
    sb = jnp.concatenate([zh, sin, zeros], axis=1)
    return cf, sa, sb


def mm_rope(a, b, tables, S):
    T = a.shape[0]
    tm = _pick(min(T, S), 1024)
    spt = S // tm
    spec = lambda tm_, tn_: pl.BlockSpec((tm_, LANES), lambda i, j: (i % spt, 0))
    return _mm_call(_mm_rope_kernel, a, b, BF16, extra=tables, extra_specs=(spec,) * 3,
                    tm_pref=tm, name="mm_rope")


def mm_residual(a, b, x, gate, S):
    T = a.shape[0]
    B, D = gate.shape
    tm = _pick(min(T, S), 1024)
    spt = S // tm
    xspec = lambda tm_, tn_: pl.BlockSpec((tm_, tn_), lambda i, j: (i, j))
    gspec = lambda tm_, tn_: pl.BlockSpec((1, 1, tn_), lambda i, j: (i // spt, 0, j))
    return _mm_call(_mm_residual_kernel, a, b, F32, extra=(x, gate.reshape(B, 1, D)),
                    extra_specs=(xspec, gspec), tm_pref=tm, name="mm_residual")


def _conv_kernel(z_ref, zp_ref, w_ref, o_ref, *, tm, S):
    i = pl.program_id(0)
    W = CONV_WIDTH
    z = z_ref[...].astype(F32)
    cb, u = z[:, :W], z[:, W:2 * W] * z[:, 2 * W:]
    zp = zp_ref[...].astype(F32)
    up = zp[:, W:2 * W] * zp[:, 2 * W:]
    up = jnp.where((i * tm) % S == 0, 0.0, up)
    row = lax.broadcasted_iota(I32, u.shape, 0)
    last, prev = up[SUBLANES - 1:SUBLANES], up[SUBLANES - 2:SUBLANES - 1]
    u1 = jnp.where(row == 0, last, pltpu.roll(u, 1, 0))
    u2 = jnp.where(row == 0, prev, jnp.where(row == 1, last, pltpu.roll(u, 2, 0)))
    w = w_ref[...]
    o_ref[...] = (cb * (w[0:1] * u2 + w[1:2] * u1 + w[2:3] * u)).astype(o_ref.dtype)


def gated_conv(z, conv_w, S):
    T = z.shape[0]
    W = CONV_WIDTH
    assert CONV_K == 3
    tm = _pick(S, 512)
    wp = jnp.zeros((SUBLANES, W), F32).at[:CONV_K].set(conv_w)
    return pl.pallas_call(
        functools.partial(_conv_kernel, tm=tm, S=S),
        grid=(T // tm,),
        in_specs=[pl.BlockSpec((tm, 3 * W), lambda i: (i, 0)),
                  pl.BlockSpec((SUBLANES, 3 * W), lambda i: (jnp.maximum(i * (tm // SUBLANES) - 1, 0), 0)),
                  pl.BlockSpec((SUBLANES, W), lambda i: (0, 0))],
        out_specs=pl.BlockSpec((tm, W), lambda i: (i, 0)),
        out_shape=jax.ShapeDtypeStruct((T, W), BF16),
        compiler_params=_params("parallel"),
        name="gated_conv",
    )(z, z, wp)


def _compress_kernel(x_ref, w1_ref, pe_ref, w2_ref, o_ref):
    x = x_ref[...]
    w1 = w1_ref[...]
    half = x.shape[1]
    ns = x.shape[0]
    first = _dot(x, w1[:half].astype(BF16))
    second = _dot(x, w1[half:].astype(BF16))
    bias = _dot(pe_ref[...], w1, precision=HI)[0:1]
    hid = first + pltpu.roll(second, ns - 1, 0) + bias
    act = jax.nn.gelu(hid, approximate=True)
    o_ref[...] = _dot(act.astype(BF16), w2_ref[...].astype(BF16)).astype(o_ref.dtype)


def compress_kv(x, cmp_pe, cmp_w1, cmp_w2):
    _, B, G, NS, W = x.shape
    assert CMP_BLOCK == 2 * CMP_STRIDE
    Dh = NSA_HEAD_DIM
    hid = cmp_w1.shape[-1]
    pe = jnp.broadcast_to(cmp_pe.reshape(2, 1, CMP_BLOCK * Dh), (2, SUBLANES, CMP_BLOCK * Dh))
    return pl.pallas_call(
        _compress_kernel,
        grid=(2, B, G),
        in_specs=[pl.BlockSpec((None, None, None, NS, W), lambda s, b, g: (s, b, g, 0, 0)),
                  pl.BlockSpec((None, CMP_BLOCK * Dh, hid), lambda s, b, g: (s, 0, 0)),
                  pl.BlockSpec((None, SUBLANES, CMP_BLOCK * Dh), lambda s, b, g: (s, 0, 0)),
                  pl.BlockSpec((None, hid, Dh), lambda s, b, g: (s, 0, 0))],
        out_specs=pl.BlockSpec((None, None, None, NS, Dh), lambda s, b, g: (s, b, g, 0, 0)),
        out_shape=jax.ShapeDtypeStruct((2, B, G, NS, Dh), BF16),
        compiler_params=_params("parallel", "parallel", "parallel"),
        name="nsa_compress",
    )(x, cmp_w1, pe, cmp_w2)


def _nsa_kernel(q_ref, kc_ref, vc_ref, ks_ref, vs_ref, kw_ref, vw_ref, zg_ref, bmap_ref, o_ref,
                m_sc, l_sc, acc_sc, *, tq, tk, nsel):
    qi = pl.program_id(2)
    t0 = qi * tq
    Dh = NSA_HEAD_DIM
    hpg = NSA_HPG
    scale = Dh ** -0.5
    q4 = q_ref[...]
    qs = jnp.concatenate([q4[:, h * Dh:(h + 1) * Dh] for h in range(hpg)], axis=0)
    trow = t0 + lax.broadcasted_iota(I32, (tq, 1), 0)

    def rep(a):
        return jnp.concatenate([a] * hpg, axis=0)

    kc = kc_ref[...]
    ncb = kc.shape[0]
    s = _dot_nt(qs, kc) * scale
    blk_end = lax.broadcasted_iota(I32, (1, ncb), 1) * CMP_STRIDE + (CMP_BLOCK - 1)
    bias_c = rep(jnp.where(blk_end <= trow, 0.0, NEG_INF))
    validf = jnp.where(bias_c == 0.0, 1.0, 0.0)
    sm = s + bias_c
    p = jnp.exp(sm - jnp.max(sm, axis=1, keepdims=True)) * validf
    pc = p / jnp.maximum(jnp.sum(p, axis=1, keepdims=True), 1e-30)
    o_c = _dot(pc.astype(BF16), vc_ref[...])

    psum = pc[0:tq]
    for h in range(1, hpg):
        psum = psum + pc[h * tq:(h + 1) * tq]
    imp = _dot(psum, bmap_ref[...], precision=HI)
    nb = imp.shape[1]
    jb = lax.broadcasted_iota(I32, (tq, nb), 1)
    jbf = jb.astype(F32)
    cur = trow // SEL_BLOCK
    forced = (jb == 0) | (jb == cur) | (jb == cur - 1)
    imp = jnp.where(forced, FORCE_SCORE, imp)
    imp = jnp.where(jb <= cur, imp, NEG_INF)
    sel = jnp.zeros((tq, nb), F32)
    for _ in range(nsel):
        mx = jnp.max(imp, axis=1, keepdims=True)
        first = jnp.min(jnp.where(imp == mx, jbf, float(nb)), axis=1, keepdims=True)
        hit = jbf == first
        sel = jnp.where(hit, 1.0, sel)
        imp = jnp.where(hit, -3e38, imp)
    selb = sel.astype(BF16)

    m_sc[...] = jnp.full(m_sc.shape, NEG_INF, F32)
    l_sc[...] = jnp.zeros(l_sc.shape, F32)
    acc_sc[...] = jnp.zeros(acc_sc.shape, F32)
    bpt = tk // SEL_BLOCK

    def sweep(kt, carry):
        k0 = pl.multiple_of(kt * tk, tk)
        k = ks_ref[pl.ds(k0, tk), :]
        v = vs_ref[pl.ds(k0, tk), :]
        expand = (lax.broadcasted_iota(I32, (nb, tk), 0)
                  == kt * bpt + lax.broadcasted_iota(I32, (nb, tk), 1) // SEL_BLOCK)
        chosen = _dot(selb, jnp.where(expand, 1.0, 0.0).astype(BF16))
        kpos = k0 + lax.broadcasted_iota(I32, (1, tk), 1)
        bias = rep(jnp.where((chosen > 0.5) & (kpos <= trow), 0.0, NEG_INF))
        sk = _dot_nt(qs, k) * scale + bias
        m_old = m_sc[...]
        m_new = jnp.maximum(m_old, jnp.max(sk, axis=1, keepdims=True))
        alpha = jnp.exp(m_old - m_new)
        pk = jnp.exp(sk - m_new)
        l_sc[...] = alpha * l_sc[...] + jnp.sum(pk, axis=1, keepdims=True)
        acc_sc[...] = alpha * acc_sc[...] + _dot(pk.astype(BF16), v)
        m_sc[...] = m_new
        return carry

    lax.fori_loop(0, (t0 + tq + tk - 1) // tk, sweep, 0)
    o_s = acc_sc[...] / l_sc[...]

    span = WINDOW + tq
    w0 = pl.multiple_of(jnp.maximum(t0 - WINDOW, 0), tq)
    kwv = kw_ref[pl.ds(w0, span), :]
    vwv = vw_ref[pl.ds(w0, span), :]
    diff = trow - (w0 + lax.broadcasted_iota(I32, (1, span), 1))
    bias_w = rep(jnp.where((diff >= 0) & (diff < WINDOW), 0.0, NEG_INF))
    sw = _dot_nt(qs, kwv) * scale + bias_w
    pw = jnp.exp(sw - jnp.max(sw, axis=1, keepdims=True))
    o_w = _dot(pw.astype(BF16), vwv) / jnp.sum(pw, axis=1, keepdims=True)

    gates = jax.nn.sigmoid(zg_ref[...])
    for h in range(hpg):
        rows = slice(h * tq, (h + 1) * tq)
        o = (gates[:, 3 * h:3 * h + 1] * o_c[rows] + gates[:, 3 * h + 1:3 * h + 2] * o_s[rows]
             + gates[:, 3 * h + 2:3 * h + 3] * o_w[rows])
        o_ref[:, h * Dh:(h + 1) * Dh] = o.astype(o_ref.dtype)


def block_map(nc, nb):
    i = np.arange(nc)[:, None] * CMP_STRIDE
    j = np.arange(nb)[None, :] * SEL_BLOCK
    inter = np.clip(np.minimum(i + CMP_BLOCK, j + SEL_BLOCK) - np.maximum(i, j), 0, None)
    return jnp.asarray(inter.astype(np.float32) / np.float32(CMP_BLOCK))


def nsa_attention(zqk, zv, zgate, kv_cmp, B, S, qk_col0, v_col0, gate_col0):
    T = B * S
    G, Dh, hpg = NSA_KV_GROUPS, NSA_HEAD_DIM, NSA_HPG
    GW = G * Dh
    tq = 128
    tk = _pick(S, 256)
    assert S % tq == 0 and WINDOW % tq == 0 and S >= WINDOW + tq and tk % SEL_BLOCK == 0
    NS = kv_cmp.shape[3]
    nb = S // SEL_BLOCK
    nq = S // tq
    bmap = block_map(NS, nb)
    qb0 = qk_col0 // (hpg * Dh)
    ks_b = (qk_col0 + NSA_Q_WIDTH + GW) // Dh
    kw_b = (qk_col0 + NSA_Q_WIDTH + 2 * GW) // Dh
    vs_b = (v_col0 + GW) // Dh
    vw_b = (v_col0 + 2 * GW) // Dh
    g_b = gate_col0 // LANES
    R = hpg * tq
    cmp_spec = lambda s: pl.BlockSpec((None, None, None, NS, Dh), lambda b, g, i: (s, b, g, 0, 0))
    seq_spec = lambda blk: pl.BlockSpec((S, Dh), lambda b, g, i: (b, blk + g))
    return pl.pallas_call(
        functools.partial(_nsa_kernel, tq=tq, tk=tk, nsel=min(N_SELECT, nb)),
        grid=(B, G, nq),
        in_specs=[pl.BlockSpec((tq, hpg * Dh), lambda b, g, i: (b * nq + i, qb0 + g)),
                  cmp_spec(0), cmp_spec(1),
                  seq_spec(ks_b), seq_spec(vs_b), seq_spec(kw_b), seq_spec(vw_b),
                  pl.BlockSpec((tq, LANES), lambda b, g, i: (b * nq + i, g_b + g)),
                  pl.BlockSpec((NS, nb), lambda b, g, i: (0, 0))],
        out_specs=pl.BlockSpec((tq, hpg * Dh), lambda b, g, i: (b * nq + i, g)),
        out_shape=jax.ShapeDtypeStruct((T, NSA_Q_WIDTH), BF16),
        scratch_shapes=[pltpu.VMEM((R, 1), F32), pltpu.VMEM((R, 1), F32), pltpu.VMEM((R, Dh), F32)],
        compiler_params=_params("parallel", "parallel", "arbitrary"),
        name="nsa_attention",
    )(zqk, kv_cmp, kv_cmp, zqk, zv, zqk, zv, zgate, bmap)


def _rwkv_prep_kernel(u_ref, up_ref, mu_ref, w0_ref, a0_ref, wb_ref, ab_ref, gb_ref,
                      r_ref, k_ref, v_ref, ld_ref, a_ref, g_ref, *, tm, S):
    i = pl.program_id(0)
    W = RWKV_WIDTH
    u = u_ref[...]
    prev = jnp.where((i * tm) % S == 0, 0.0, up_ref[...][SUBLANES - 1:SUBLANES])
    row = lax.broadcasted_iota(I32, u.shape, 0)
    u_prev = jnp.where(row == 0, prev, pltpu.roll(u, 1, 0))
    um = u + (u_prev - u) * mu_ref[...]
    r_ref[...] = um[:, :W]
    k_ref[...] = um[:, W:2 * W]
    v_ref[...] = um[:, 2 * W:3 * W]
    lora = um[:, RWKV_TAIL:RWKV_TAIL + LANES]
    gl = um[:, RWKV_TAIL + LANES:RWKV_TAIL + LANES + gb_ref.shape[0]]
    y = w0_ref[...] + _dot(jnp.tanh(lora), wb_ref[...], precision=HI)
    ld_ref[...] = -float(np.exp(-0.5)) * jax.nn.sigmoid(y)
    a_ref[...] = jax.nn.sigmoid(a0_ref[...] + _dot(lora, ab_ref[...], precision=HI))
    g_ref[...] = _dot(jax.nn.sigmoid(gl), gb_ref[...], precision=HI)


def rwkv_prep(zr, rwkv_mu, w0, wb, a0, ab, gb, S):
    T = zr.shape[0]
    W, PW = RWKV_WIDTH, RWKV_PAD_WIDTH
    assert RWKV_DECAY_LORA + RWKV_A_LORA == LANES
    glw = -(-RWKV_GATE_LORA // LANES) * LANES
    assert RWKV_TAIL + LANES + glw <= PW
    tm = _pick(S, 256)
    mu = jnp.zeros((1, PW), F32).at[0, :RWKV_IN_WIDTH].set(rwkv_mu)
    wbp = jnp.zeros((LANES, W), F32).at[:RWKV_DECAY_LORA].set(wb)
    abp = jnp.zeros((LANES, W), F32).at[RWKV_DECAY_LORA:].set(ab)
    gbp = jnp.zeros((glw, W), F32).at[:RWKV_GATE_LORA].set(gb)
    row_spec = pl.BlockSpec((1, W), lambda i: (0, 0))
    out_spec = pl.BlockSpec((tm, W), lambda i: (i, 0))
    return pl.pallas_call(
        functools.partial(_rwkv_prep_kernel, tm=tm, S=S),
        grid=(T // tm,),
        in_specs=[pl.BlockSpec((tm, PW), lambda i: (i, 0)),
                  pl.BlockSpec((SUBLANES, PW), lambda i: (jnp.maximum(i * (tm // SUBLANES) - 1, 0), 0)),
                  pl.BlockSpec((1, PW), lambda i: (0, 0)),
                  row_spec, row_spec,
                  pl.BlockSpec((LANES, W), lambda i: (0, 0)),
                  pl.BlockSpec((LANES, W), lambda i: (0, 0)),
                  pl.BlockSpec((glw, W), lambda i: (0, 0))],
        out_specs=[out_spec] * 6,
        out_shape=[jax.ShapeDtypeStruct((T, W), F32)] * 6,
        compiler_params=_params("parallel"),
        name="rwkv_prep",
    )(zr, zr, mu, w0.reshape(1, W), a0.reshape(1, W), wbp, abp, gbp)


def _rwkv_kernel(r_ref, k_ref, v_ref, ld_ref, a_ref, g_ref, kks_ref, ka_ref, rk_ref, lnw_ref, lnb_ref,
                 o_ref, h_sc, *, pairs, prec):
    C = RWKV_CHUNK
    N = RWKV_HEAD_DIM
    P2 = 2 * C

    @pl.when(pl.program_id(2) == 0)
    def _():
        h_sc[...] = jnp.zeros(h_sc.shape, F32)

    row = lax.broadcasted_iota(I32, (P2, LANES), 0)
    col = lax.broadcasted_iota(I32, (P2, LANES), 1)
    same = (row // C) == (col // N)
    sameq = (row // C) == (col // C)
    incl = sameq & (col <= row)
    strict = sameq & (col < row)
    eye = row == col
    grp = jnp.where((row // N) == (col // N), 1.0, 0.0)
    tr = lax.broadcasted_iota(I32, (C, C), 0)
    tc = lax.broadcasted_iota(I32, (C, C), 1)
    lower = jnp.where(tc <= tr, 1.0, 0.0)

    def mm(a, b):
        return _dot(a, b, precision=prec)

    def stack(x):
        return jnp.where(same, jnp.concatenate([x, x], axis=0), 0.0)

    for p in range(pairs):
        sl = slice(p * LANES, (p + 1) * LANES)
        r, kraw, v, ld, a = r_ref[:, sl], k_ref[:, sl], v_ref[:, sl], ld_ref[:, sl], a_ref[:, sl]
        kk = kraw * kks_ref[:, sl]
        kk = kk / jnp.maximum(jnp.sqrt(_dot(kk * kk, grp, precision=HI)), 1e-12)
        k = kraw * (1.0 + (a - 1.0) * ka_ref[:, sl])
        cum = _dot(lower, ld, precision=HI)
        tot = cum[C - 1:C]
        e_in, e_ex, e_inv, e_rem = jnp.exp(cum), jnp.exp(cum - ld), jnp.exp(-cum), jnp.exp(tot - cum)
        beta = kk * a
        a_s, r_s = stack(-kk * e_ex), stack(r * e_in)
        bt_s, kt_s = stack(beta * e_inv), stack(k * e_inv)
        bh_s, kh_s = stack(beta * e_rem), stack(k * e_rem)
        v_s = stack(v)
        quad = _dot_nt(jnp.concatenate([a_s, r_s], axis=0), jnp.concatenate([bt_s, kt_s], axis=0),
                       precision=prec)
        l_ab = jnp.where(strict, quad[:P2, :P2], 0.0)
        l_ak = jnp.where(strict, quad[:P2, P2:], 0.0)
        m_rb = jnp.where(incl, quad[P2:, :P2], 0.0)
        m_rk = jnp.where(incl, quad[P2:, P2:], 0.0)
        x = jnp.concatenate([a_s, mm(l_ak, v_s)], axis=1)
        lp = l_ab
        steps = int(np.log2(C))
        for s in range(steps):
            x = x + mm(lp, x)
            if s + 1 < steps:
                lp = mm(lp, lp)
        h = h_sc[p]
        qo = mm(m_rb, x)
        o = mm(r_s + qo[:, :LANES], h) + qo[:, LANES:] + mm(m_rk, v_s)
        pg = _dot_tn(bh_s, x, precision=prec)
        trans = jnp.where(eye, jnp.exp(tot), 0.0) + pg[:, :LANES]
        h_sc[p] = mm(trans, h) + pg[:, LANES:] + _dot_tn(kh_s, v_s, precision=prec)
        o = o[:C] + o[C:]
        mean = _dot(o, grp, precision=HI) * (1.0 / N)
        d = o - mean
        var = _dot(d * d, grp, precision=HI) * (1.0 / N)
        y = d * lax.rsqrt(var + RWKV_LN_EPS) * lnw_ref[:, sl] + lnb_ref[:, sl]
        y = y + _dot(r * k * rk_ref[:, sl], grp, precision=HI) * v
        o_ref[:, sl] = (y * g_ref[:, sl]).astype(o_ref.dtype)


def rwkv_mix(r, k, v, ld, a, g, kk_scale, ka, rk, ln_w, ln_b, B, S, prec=HI):
    T, W = r.shape
    C = RWKV_CHUNK
    assert 2 * RWKV_HEAD_DIM == LANES and 2 * C == LANES and S % C == 0
    pairs = 4
    bw = pairs * LANES
    nc = S // C
    seq = pl.BlockSpec((C, bw), lambda b, p, c: (b * nc + c, p))
    par = pl.BlockSpec((1, bw), lambda b, p, c: (0, p))
    return pl.pallas_call(
        functools.partial(_rwkv_kernel, pairs=pairs, prec=prec),
        grid=(B, W // bw, nc),
        in_specs=[seq] * 6 + [par] * 5,
        out_specs=seq,
        out_shape=jax.ShapeDtypeStruct((T, W), BF16),
        scratch_shapes=[pltpu.VMEM((pairs, LANES, LANES), F32)],
        compiler_params=_params("parallel", "parallel", "arbitrary"),
        name="rwkv_mix",
    )(r, k, v, ld, a, g, *(z.reshape(1, W) for z in (kk_scale, ka, rk, ln_w, ln_b)))


def _merge_kernel(ya_ref, yb_ref, yc_ref, pa_ref, pb_ref, pc_ref, ga_ref, gb_ref, gc_ref, o_ref):
    m = ga_ref[...].astype(F32) * _dot(ya_ref[...], pa_ref[...])
    m = m + gb_ref[...].astype(F32) * _dot(yb_ref[...], pb_ref[...])
    m = m + gc_ref[...].astype(F32) * _dot(yc_ref[...], pc_ref[...])
    o_ref[...] = m.astype(o_ref.dtype)


def branch_merge(ya, yb, yc, pa, pb, pc, gm, D):
    T = ya.shape[0]
    tm, tn = _pick(T, 1024), _pick(D, 512)
    nj = D // tn
    ysp = lambda y: pl.BlockSpec((tm, y.shape[1]), lambda i, j: (i, 0))
    psp = lambda p: pl.BlockSpec((p.shape[0], tn), lambda i, j: (0, j))
    gsp = lambda n: pl.BlockSpec((tm, tn), lambda i, j: (i, n * nj + j))
    return pl.pallas_call(
        _merge_kernel,
        grid=(T // tm, nj),
        in_specs=[ysp(ya), ysp(yb), ysp(yc), psp(pa), psp(pb), psp(pc), gsp(0), gsp(1), gsp(2)],
        out_specs=pl.BlockSpec((tm, tn), lambda i, j: (i, j)),
        out_shape=jax.ShapeDtypeStruct((T, D), BF16),
        compiler_params=_params("parallel", "arbitrary"),
        name="branch_merge",
    )(ya, yb, yc, pa, pb, pc, gm, gm, gm)


def _moe_kernel(blk_e_ref, asg_ref, h_hbm, wg_ref, bg_ref, wu_ref, bu_ref, wd_ref, bd_ref, y_hbm,
                xbuf, ybuf, gsem, ssem, *, R):
    base = pl.program_id(0) * R

    def gather(r):
        tok = jnp.maximum(asg_ref[base + r], 0) // TOP_K
        return pltpu.make_async_copy(h_hbm.at[pl.ds(tok, 1)], xbuf.at[pl.ds(r, 1)], gsem)

    def scatter(r):
        return pltpu.make_async_copy(ybuf.at[pl.ds(r, 1)], y_hbm.at[pl.ds(asg_ref[base + r], 1)], ssem)

    def each(fn):
        def body(r, c):
            fn(r)
            return c
        lax.fori_loop(0, R, body, 0)

    each(lambda r: gather(r).start())
    each(lambda r: gather(r).wait())
    x = xbuf[...].astype(BF16)
    gt = jnp.minimum(_dot(x, wg_ref[...]) + bg_ref[...], SWIGLU_LIMIT)
    up = jnp.clip(_dot(x, wu_ref[...]) + bu_ref[...], -SWIGLU_LIMIT, SWIGLU_LIMIT)
    act = gt * jax.nn.sigmoid(SWIGLU_ALPHA * gt) * (up + 1.0)
    ybuf[...] = _dot(act.astype(BF16), wd_ref[...]) + bd_ref[...]

    def live(r, fn):
        @pl.when(asg_ref[base + r] >= 0)
        def _():
            fn(r)

    each(lambda r: live(r, lambda q: scatter(q).start()))
    each(lambda r: live(r, lambda q: scatter(q).wait()))


def moe_experts(h, blk_e, row_asg, wg, bg, wu, bu, wd, bd, R):
    T, D = h.shape
    E, _, F = wg.shape
    n_blk = blk_e.shape[0]
    wspec = lambda shape: pl.BlockSpec((None,) + shape, lambda i, be, ra: (be[i], 0, 0))
    return pl.pallas_call(
        functools.partial(_moe_kernel, R=R),
        grid_spec=pltpu.PrefetchScalarGridSpec(
            num_scalar_prefetch=2,
            grid=(n_blk,),
            in_specs=[pl.BlockSpec(memory_space=pl.ANY),
                      wspec((D, F)), wspec((1, F)), wspec((D, F)), wspec((1, F)),
                      wspec((F, D)), wspec((1, D))],
            out_specs=pl.BlockSpec(memory_space=pl.ANY),
            scratch_shapes=[pltpu.VMEM((R, D), F32), pltpu.VMEM((R, D), F32),
                            pltpu.SemaphoreType.DMA, pltpu.SemaphoreType.DMA]),
        out_shape=jax.ShapeDtypeStruct((T * TOP_K, D), F32),
        compiler_params=_params("arbitrary"),
        name="moe_experts",
    )(blk_e, row_asg, h, wg, bg.reshape(E, 1, F), wu, bu.reshape(E, 1, F), wd, bd.reshape(E, 1, D))


def _combine_kernel(x_ref, y_ref, wt_ref, gate_ref, o_ref, *, D):
    wt = wt_ref[...]
    acc = wt[:, 0:1] * y_ref[:, 0:D]
    for k in range(1, TOP_K):
        acc = acc + wt[:, k:k + 1] * y_ref[:, k * D:(k + 1) * D]
    o_ref[...] = x_ref[...] + gate_ref[0] * acc


def moe_combine(x, y, wt, gate, S):
    T, D = x.shape
    B = gate.shape[0]
    tm = _pick(S, 128)
    spt = S // tm
    return pl.pallas_call(
        functools.partial(_combine_kernel, D=D),
        grid=(T // tm,),
        in_specs=[pl.BlockSpec((tm, D), lambda i: (i, 0)),
                  pl.BlockSpec((tm, TOP_K * D), lambda i: (i, 0)),
                  pl.BlockSpec((tm, LANES), lambda i: (i, 0)),
                  pl.BlockSpec((1, 1, D), lambda i: (i // spt, 0, 0))],
        out_specs=pl.BlockSpec((tm, D), lambda i: (i, 0)),
        out_shape=jax.ShapeDtypeStruct((T, D), F32),
        compiler_params=_params("parallel"),
        name="moe_combine",
    )(x, y.reshape(T, TOP_K * D), wt, gate.reshape(B, 1, D))


def route_tables(top_idx, E, R):
    T = top_idx.shape[0]
    A = T * TOP_K
    flat_e = top_idx.reshape(A)
    onehot = (flat_e[:, None] == jnp.arange(E, dtype=I32)[None, :]).astype(I32)
    rank = jnp.take_along_axis(jnp.cumsum(onehot, axis=0) - onehot, flat_e[:, None], axis=1)[:, 0]
    counts = jnp.sum(onehot, axis=0)
    padded = (counts + R - 1) // R * R
    pends = jnp.cumsum(padded)
    dest = (pends - padded)[flat_e] + rank
    n_blk = -(-(A + E * (R - 1)) // R)
    row_asg = jnp.full((n_blk * R,), -1, I32).at[dest].set(jnp.arange(A, dtype=I32))
    blk_e = jnp.minimum(jnp.searchsorted(pends, jnp.arange(n_blk, dtype=I32) * R, side='right'),
                        E - 1).astype(I32)
    return blk_e, row_asg


def _mixer_weights(w_in, D):
    CW, QW, GW = CONV_WIDTH, NSA_Q_WIDTH, NSA_G_WIDTH
    o_q = 3 * CW
    o_kv = o_q + QW
    o_g = o_kv + 6 * GW
    o_r = o_g + NSA_GATE_WIDTH
    o_m = o_r + RWKV_IN_WIDTH
    kv = lambda j: w_in[:, o_kv + j * GW:o_kv + (j + 1) * GW]
    w_qk = jnp.concatenate([w_in[:, o_q:o_kv], kv(0), kv(2), kv(4)], axis=1).astype(BF16)
    w_cv = jnp.concatenate([w_in[:, :o_q], kv(1), kv(3), kv(5)], axis=1).astype(BF16)
    per_group = NSA_GATE_WIDTH // NSA_KV_GROUPS
    gate_blocks = [jnp.pad(w_in[:, o_g + g * per_group:o_g + (g + 1) * per_group],
                           ((0, 0), (0, LANES - per_group))) for g in range(NSA_KV_GROUPS)]
    w_rg = jnp.concatenate(
        [jnp.pad(w_in[:, o_r:o_m], ((0, 0), (0, RWKV_PAD_WIDTH - RWKV_IN_WIDTH)))] + gate_blocks,
        axis=1).astype(BF16)
    w_m = w_in[:, o_m:].astype(BF16)
    return w_qk, w_cv, w_rg, w_m


def _layer(x, mod, B, S, tables, norm_g, w_in, conv_w, cmp_pe, cmp_w1, cmp_w2, rwkv_mu, rwkv_w0,
           rwkv_wb, rwkv_a0, rwkv_ab, rwkv_gb, rwkv_kk, rwkv_ka, rwkv_rk, rwkv_ln_w, rwkv_ln_b,
           w_branch, w_out, router_w, router_b, exp_wg, exp_bg, exp_wu, exp_bu, exp_wd, exp_bd):
    T, D = x.shape
    CW, QW, GW, Dh, G = CONV_WIDTH, NSA_Q_WIDTH, NSA_G_WIDTH, NSA_HEAD_DIM, NSA_KV_GROUPS
    shift1, scale1, gate1, shift2, scale2, gate2 = (mod[:, n] for n in range(N_MOD))

    h = norm_modulate(x, norm_g[0], scale1, shift1, S, BF16)
    w_qk, w_cv, w_rg, w_m = _mixer_weights(w_in, D)
    zqk = mm_rope(h, w_qk, tables, S)
    zcv = _mm_call(_mm_plain_kernel, h, w_cv, BF16, name="mm_conv_v")
    zrg = _mm_call(_mm_plain_kernel, h, w_rg, F32, name="mm_rwkv_gate")
    gm = _mm_call(_mm_sigmoid_kernel, h, w_m, BF16, name="mm_merge_gate")

    y_a = gated_conv(zcv, conv_w, S)

    ns = S // CMP_STRIDE
    to_strides = lambda z: z.reshape(B, ns, CMP_STRIDE, G, Dh).transpose(0, 3, 1, 2, 4).reshape(
        B, G, ns, CMP_STRIDE * Dh)
    cmp_in = jnp.stack([to_strides(zqk[:, QW:QW + GW]), to_strides(zcv[:, 3 * CW:3 * CW + GW])])
    kv_cmp = compress_kv(cmp_in, cmp_pe, cmp_w1, cmp_w2)
    y_b = nsa_attention(zqk, zcv, zrg, kv_cmp, B, S, qk_col0=0, v_col0=3 * CW, gate_col0=RWKV_PAD_WIDTH)

    r, k, v, ld, a, g = rwkv_prep(zrg, rwkv_mu, rwkv_w0, rwkv_wb, rwkv_a0, rwkv_ab, rwkv_gb, S)
    y_c = rwkv_mix(r, k, v, ld, a, g, rwkv_kk, rwkv_ka, rwkv_rk.reshape(-1), rwkv_ln_w, rwkv_ln_b, B, S)

    wb16 = w_branch.astype(BF16)
    m = branch_merge(y_a, y_b, y_c, wb16[:CW], wb16[CW:CW + QW], wb16[CW + QW:], gm, D)
    x = mm_residual(m, w_out.astype(BF16), x, gate1, S)

    E = router_w.shape[1]
    h2, top_idx, top_w = norm_router(x, norm_g[1], scale2, shift2, router_w, router_b, S)
    R = _pick(T * TOP_K // E, 256)
    blk_e, row_asg = route_tables(top_idx[:, :TOP_K], E, R)
    y = moe_experts(h2, blk_e, row_asg, exp_wg.astype(BF16), exp_bg, exp_wu.astype(BF16), exp_bu,
                    exp_wd.astype(BF16), exp_bd, R)
    return moe_combine(x, y, top_w, gate2, S)


def kernel(x, c, ada_w, ada_b, ada_table, norm_g, final_g, w_in, conv_w, cmp_pe, cmp_w1, cmp_w2, rwkv_mu, rwkv_w0, rwkv_wb, rwkv_a0, rwkv_ab, rwkv_gb, rwkv_kk, rwkv_ka, rwkv_rk, rwkv_ln_w, rwkv_ln_b, w_branch, w_out, router_w, router_b, exp_wg, exp_bg, exp_wu, exp_bu, exp_wd, exp_bd):
    B, S, D = x.shape
    depth = w_in.shape[0]
    tables = rope_lane_tables(S)
    mod_all = ada_modulation(c, ada_w, ada_b).reshape(B, N_MOD, D)
    xf = x.reshape(B * S, D)
    for l in range(depth):
        xf = _layer(xf, mod_all + ada_table[l], B, S, tables, norm_g[l], w_in[l], conv_w[l], cmp_pe[l],
                    cmp_w1[l], cmp_w2[l], rwkv_mu[l], rwkv_w0[l], rwkv_wb[l], rwkv_a0[l], rwkv_ab[l],
                    rwkv_gb[l], rwkv_kk[l], rwkv_ka[l], rwkv_rk[l], rwkv_ln_w[l], rwkv_ln_b[l],
                    w_branch[l], w_out[l], router_w[l], router_b[l], exp_wg[l], exp_bg[l], exp_wu[l],
                    exp_bu[l], exp_wd[l], exp_bd[l])
    return final_norm(xf, final_g).reshape(B, S, D)
```

```python
import functools

import numpy as np
import jax
import jax.numpy as jnp
from jax import lax
from jax.experimental import pallas as pl
from jax.experimental.pallas import tpu as pltpu

F32 = jnp.float32
BF16 = jnp.bfloat16
I32 = jnp.int32
HI = lax.Precision.HIGHEST

N_MOD = 6
RMS_EPS = 1e-6
CONV_WIDTH = 1024
CONV_K = 3
NSA_HEADS = 16
NSA_KV_GROUPS = 4
NSA_HPG = NSA_HEADS // NSA_KV_GROUPS
NSA_HEAD_DIM = 128
CMP_BLOCK = 32
CMP_STRIDE = 16
SEL_BLOCK = 64
N_SELECT = 16
WINDOW = 512
FORCE_SCORE = 1e9
NEG_INF = -1e30
ROPE_THETA = 500000.0
ROPE_DIMS = NSA_HEAD_DIM // 4
NSA_Q_WIDTH = NSA_HEADS * NSA_HEAD_DIM
NSA_G_WIDTH = NSA_KV_GROUPS * NSA_HEAD_DIM
NSA_GATE_WIDTH = 3 * NSA_HEADS
RWKV_HEADS = 16
RWKV_HEAD_DIM = 64
RWKV_WIDTH = RWKV_HEADS * RWKV_HEAD_DIM
RWKV_DECAY_LORA = 64
RWKV_A_LORA = 64
RWKV_GATE_LORA = 160
RWKV_IN_WIDTH = 3 * RWKV_WIDTH + RWKV_DECAY_LORA + RWKV_A_LORA + RWKV_GATE_LORA
RWKV_LN_EPS = 64e-5
RWKV_CHUNK = 64
TOP_K = 4
TOP_K_SHIFT = 2
SWIGLU_ALPHA = 1.702
SWIGLU_LIMIT = 7.0

LANES = 128
SUBLANES = 8
VMEM_BYTES_V7X = 64 * 1024 * 1024
VMEM_LIMIT = VMEM_BYTES_V7X - 12 * 1024 * 1024

RWKV_PAD_WIDTH = -(-RWKV_IN_WIDTH // 512) * 512
RWKV_TAIL = 3 * RWKV_WIDTH


def _params(*sem):
    return pltpu.CompilerParams(dimension_semantics=sem, vmem_limit_bytes=VMEM_LIMIT)


def _pick(n, pref):
    t = min(n, pref)
    while n % t:
        t //= 2
    return t


def _dot(a, b, **kw):
    return jnp.dot(a, b, preferred_element_type=F32, **kw)


def _dot_nt(a, b, **kw):
    return lax.dot_general(a, b, (((1,), (1,)), ((), ())), preferred_element_type=F32, **kw)


def _dot_tn(a, b, **kw):
    return lax.dot_general(a, b, (((0,), (0,)), ((), ())), preferred_element_type=F32, **kw)


def _ada_kernel(c_ref, w_ref, b_ref, o_ref):
    c = c_ref[...]
    sc = c * jax.nn.sigmoid(c)
    o_ref[...] = _dot(sc, w_ref[...], precision=HI) + b_ref[...]


def ada_modulation(c, ada_w, ada_b):
    B, D = c.shape
    N = ada_w.shape[1]
    rows = -(-B // SUBLANES) * SUBLANES
    cp = jnp.zeros((rows, D), F32).at[:B].set(c)
    tn = _pick(N, 512)
    out = pl.pallas_call(
        _ada_kernel,
        grid=(N // tn,),
        in_specs=[pl.BlockSpec((rows, D), lambda j: (0, 0)),
                  pl.BlockSpec((D, tn), lambda j: (0, j)),
                  pl.BlockSpec((1, tn), lambda j: (0, j))],
        out_specs=pl.BlockSpec((rows, tn), lambda j: (0, j)),
        out_shape=jax.ShapeDtypeStruct((rows, N), F32),
        compiler_params=_params("parallel"),
        name="ada_mod",
    )(cp, ada_w, ada_b.reshape(1, N))
    return out[:B]


def _norm_mod(x, g, sc, sh):
    ms = jnp.mean(x * x, axis=-1, keepdims=True)
    y = x * lax.rsqrt(ms + RMS_EPS) * g
    return y * (1.0 + sc) + sh


def _norm_mod_kernel(x_ref, g_ref, sc_ref, sh_ref, o_ref):
    h = _norm_mod(x_ref[...], g_ref[...], sc_ref[0], sh_ref[0])
    o_ref[...] = h.astype(o_ref.dtype)


def norm_modulate(x, g, scale, shift, S, out_dtype):
    T, D = x.shape
    B = scale.shape[0]
    tm = _pick(S, 256)
    spt = S // tm
    return pl.pallas_call(
        _norm_mod_kernel,
        grid=(T // tm,),
        in_specs=[pl.BlockSpec((tm, D), lambda i: (i, 0)),
                  pl.BlockSpec((1, D), lambda i: (0, 0)),
                  pl.BlockSpec((1, 1, D), lambda i: (i // spt, 0, 0)),
                  pl.BlockSpec((1, 1, D), lambda i: (i // spt, 0, 0))],
        out_specs=pl.BlockSpec((tm, D), lambda i: (i, 0)),
        out_shape=jax.ShapeDtypeStruct((T, D), out_dtype),
        compiler_params=_params("parallel"),
        name="norm_mod",
    )(x, g.reshape(1, D), scale.reshape(B, 1, D), shift.reshape(B, 1, D))


def _norm_router_kernel(x_ref, g_ref, sc_ref, sh_ref, rw_ref, rb_ref, h_ref, idx_ref, wt_ref):
    h = _norm_mod(x_ref[...], g_ref[...], sc_ref[0], sh_ref[0])
    h_ref[...] = h
    logits = _dot(h, rw_ref[...], precision=HI) + rb_ref[...]
    lane = lax.broadcasted_iota(I32, logits.shape, 1)
    lanef = lane.astype(F32)
    idx_out = jnp.zeros(logits.shape, I32)
    val_out = jnp.zeros(logits.shape, F32)
    v0 = None
    for k in range(TOP_K):
        mx = jnp.max(logits, axis=1, keepdims=True)
        first = jnp.min(jnp.where(logits == mx, lanef, float(LANES)), axis=1, keepdims=True)
        hit = lanef == first
        if k == 0:
            v0 = mx
        idx_out = jnp.where(lane == k, first.astype(I32), idx_out)
        val_out = jnp.where(lane == k, jnp.exp(mx - v0), val_out)
        logits = jnp.where(hit, -3e38, logits)
    idx_ref[...] = idx_out
    wt_ref[...] = val_out / jnp.sum(val_out, axis=1, keepdims=True)


def norm_router(x, g, scale, shift, router_w, router_b, S):
    T, D = x.shape
    B = scale.shape[0]
    E = router_w.shape[1]
    assert E <= LANES
    rw = jnp.zeros((D, LANES), F32).at[:, :E].set(router_w)
    rb = jnp.full((1, LANES), NEG_INF, F32).at[0, :E].set(router_b)
    tm = _pick(S, 256)
    spt = S // tm
    return pl.pallas_call(
        _norm_router_kernel,
        grid=(T // tm,),
        in_specs=[pl.BlockSpec((tm, D), lambda i: (i, 0)),
                  pl.BlockSpec((1, D), lambda i: (0, 0)),
                  pl.BlockSpec((1, 1, D), lambda i: (i // spt, 0, 0)),
                  pl.BlockSpec((1, 1, D), lambda i: (i // spt, 0, 0)),
                  pl.BlockSpec((D, LANES), lambda i: (0, 0)),
                  pl.BlockSpec((1, LANES), lambda i: (0, 0))],
        out_specs=[pl.BlockSpec((tm, D), lambda i: (i, 0)),
                   pl.BlockSpec((tm, LANES), lambda i: (i, 0)),
                   pl.BlockSpec((tm, LANES), lambda i: (i, 0))],
        out_shape=[jax.ShapeDtypeStruct((T, D), F32),
                   jax.ShapeDtypeStruct((T, LANES), I32),
                   jax.ShapeDtypeStruct((T, LANES), F32)],
        compiler_params=_params("parallel"),
        name="norm_router",
    )(x, g.reshape(1, D), scale.reshape(B, 1, D), shift.reshape(B, 1, D), rw, rb)


def _final_norm_kernel(x_ref, g_ref, o_ref):
    x = x_ref[...]
    ms = jnp.mean(x * x, axis=-1, keepdims=True)
    o_ref[...] = x * lax.rsqrt(ms + RMS_EPS) * g_ref[...]


def final_norm(x, g):
    T, D = x.shape
    tm = _pick(T, 256)
    return pl.pallas_call(
        _final_norm_kernel,
        grid=(T // tm,),
        in_specs=[pl.BlockSpec((tm, D), lambda i: (i, 0)), pl.BlockSpec((1, D), lambda i: (0, 0))],
        out_specs=pl.BlockSpec((tm, D), lambda i: (i, 0)),
        out_shape=jax.ShapeDtypeStruct((T, D), F32),
        compiler_params=_params("parallel"),
        name="final_norm",
    )(x, g.reshape(1, D))


def _rope_lanes(x, cf, sa, sb):
    half = ROPE_DIMS // 2
    return x * cf + pltpu.roll(x, LANES - half, 1) * sa + pltpu.roll(x, half, 1) * sb


def _mm_plain_kernel(a_ref, b_ref, o_ref):
    o_ref[...] = _dot(a_ref[...], b_ref[...]).astype(o_ref.dtype)


def _mm_sigmoid_kernel(a_ref, b_ref, o_ref):
    o_ref[...] = jax.nn.sigmoid(_dot(a_ref[...], b_ref[...])).astype(o_ref.dtype)


def _mm_rope_kernel(a_ref, b_ref, cf_ref, sa_ref, sb_ref, o_ref, *, q_tiles):
    fac = jnp.where(pl.program_id(1) < q_tiles, NSA_HEAD_DIM ** -0.5, 1.0)
    acc = _dot(a_ref[...], b_ref[...]) * fac
    cf, sa, sb = cf_ref[...], sa_ref[...], sb_ref[...]
    for n in range(acc.shape[1] // LANES):
        sl = slice(n * LANES, (n + 1) * LANES)
        o_ref[:, sl] = _rope_lanes(acc[:, sl], cf, sa, sb).astype(o_ref.dtype)


def _mm_residual_kernel(a_ref, b_ref, x_ref, gate_ref, o_ref):
    o_ref[...] = x_ref[...] + gate_ref[0] * _dot(a_ref[...], b_ref[...])


def _mm_call(kernel, a, b, out_dtype, extra=(), extra_specs=(), tm_pref=1024, tn_pref=512, name="mm"):
    M, K = a.shape
    N = b.shape[1]
    tm, tn = _pick(M, tm_pref), _pick(N, tn_pref)
    return pl.pallas_call(
        kernel,
        grid=(M // tm, N // tn),
        in_specs=[pl.BlockSpec((tm, K), lambda i, j: (i, 0)),
                  pl.BlockSpec((K, tn), lambda i, j: (0, j))] + [s(tm, tn) for s in extra_specs],
        out_specs=pl.BlockSpec((tm, tn), lambda i, j: (i, j)),
        out_shape=jax.ShapeDtypeStruct((M, N), out_dtype),
        compiler_params=_params("parallel", "arbitrary"),
        name=name,
    )(a, b, *extra)


def rope_lane_tables(S):
    half = ROPE_DIMS // 2
    inv = ROPE_THETA ** (-jnp.arange(0, ROPE_DIMS, 2, dtype=F32) / ROPE_DIMS)
    ang = jnp.arange(S, dtype=F32)[:, None] * inv[None, :]
    cos, sin = jnp.cos(ang), jnp.sin(ang)
    zeros = jnp.zeros((S, LANES - ROPE_DIMS), F32)
    zh = jnp.zeros((S, half), F32)
    cf = jnp.concatenate([cos, cos, zeros + 1.0], axis=1)
    sa = jnp.concatenate([-sin, zh, zeros], axis=1)
    sb = jnp.concatenate([zh, sin, zeros], axis=1)
    return cf, sa, sb


def mm_rope(a, b, tables, S):
    T = a.shape[0]
    tm = _pick(min(T, S), 1024)
    spt = S // tm
    spec = lambda tm_, tn_: pl.BlockSpec((tm_, LANES), lambda i, j: (i % spt, 0))
    tn = _pick(NSA_Q_WIDTH, 512)
    assert b.shape[1] % tn == 0
    return _mm_call(functools.partial(_mm_rope_kernel, q_tiles=NSA_Q_WIDTH // tn), a, b, BF16,
                    extra=tables, extra_specs=(spec,) * 3, tm_pref=tm, tn_pref=tn, name="mm_rope")


def mm_residual(a, b, x, gate, S):
    T = a.shape[0]
    B, D = gate.shape
    tm = _pick(min(T, S), 1024)
    spt = S // tm
    xspec = lambda tm_, tn_: pl.BlockSpec((tm_, tn_), lambda i, j: (i, j))
    gspec = lambda tm_, tn_: pl.BlockSpec((1, 1, tn_), lambda i, j: (i // spt, 0, j))
    return _mm_call(_mm_residual_kernel, a, b, F32, extra=(x, gate.reshape(B, 1, D)),
                    extra_specs=(xspec, gspec), tm_pref=tm, name="mm_residual")


def _conv_kernel(z_ref, zp_ref, w_ref, o_ref, *, tm, S):
    i = pl.program_id(0)
    W = CONV_WIDTH
    z = z_ref[...].astype(F32)
    cb, u = z[:, :W], z[:, W:2 * W] * z[:, 2 * W:]
    zp = zp_ref[...].astype(F32)
    up = zp[:, W:2 * W] * zp[:, 2 * W:]
    up = jnp.where((i * tm) % S == 0, 0.0, up)
    row = lax.broadcasted_iota(I32, u.shape, 0)
    last, prev = up[SUBLANES - 1:SUBLANES], up[SUBLANES - 2:SUBLANES - 1]
    u1 = jnp.where(row == 0, last, pltpu.roll(u, 1, 0))
    u2 = jnp.where(row == 0, prev, jnp.where(row == 1, last, pltpu.roll(u, 2, 0)))
    w = w_ref[...]
    o_ref[...] = (cb * (w[0:1] * u2 + w[1:2] * u1 + w[2:3] * u)).astype(o_ref.dtype)


def gated_conv(z, conv_w, S):
    T = z.shape[0]
    W = CONV_WIDTH
    assert CONV_K == 3
    tm = _pick(S, 512)
    wp = jnp.zeros((SUBLANES, W), F32).at[:CONV_K].set(conv_w)
    return pl.pallas_call(
        functools.partial(_conv_kernel, tm=tm, S=S),
        grid=(T // tm,),
        in_specs=[pl.BlockSpec((tm, 3 * W), lambda i: (i, 0)),
                  pl.BlockSpec((SUBLANES, 3 * W), lambda i: (jnp.maximum(i * (tm // SUBLANES) - 1, 0), 0)),
                  pl.BlockSpec((SUBLANES, W), lambda i: (0, 0))],
        out_specs=pl.BlockSpec((tm, W), lambda i: (i, 0)),
        out_shape=jax.ShapeDtypeStruct((T, W), BF16),
        compiler_params=_params("parallel"),
        name="gated_conv",
    )(z, z, wp)


def _compress_kernel(x_ref, w1_ref, pe_ref, w2_ref, o_ref):
    x = x_ref[...]
    w1 = w1_ref[...]
    half = x.shape[1]
    ns = x.shape[0]
    first = _dot(x, w1[:half].astype(BF16))
    second = _dot(x, w1[half:].astype(BF16))
    bias = _dot(pe_ref[...], w1, precision=HI)[0:1]
    hid = first + pltpu.roll(second, ns - 1, 0) + bias
    act = jax.nn.gelu(hid, approximate=True)
    o_ref[...] = _dot(act.astype(BF16), w2_ref[...].astype(BF16)).astype(o_ref.dtype)


def compress_kv(x, cmp_pe, cmp_w1, cmp_w2):
    _, B, G, NS, W = x.shape
    assert CMP_BLOCK == 2 * CMP_STRIDE
    Dh = NSA_HEAD_DIM
    hid = cmp_w1.shape[-1]
    pe = jnp.broadcast_to(cmp_pe.reshape(2, 1, CMP_BLOCK * Dh), (2, SUBLANES, CMP_BLOCK * Dh))
    return pl.pallas_call(
        _compress_kernel,
        grid=(2, B, G),
        in_specs=[pl.BlockSpec((None, None, None, NS, W), lambda s, b, g: (s, b, g, 0, 0)),
                  pl.BlockSpec((None, CMP_BLOCK * Dh, hid), lambda s, b, g: (s, 0, 0)),
                  pl.BlockSpec((None, SUBLANES, CMP_BLOCK * Dh), lambda s, b, g: (s, 0, 0)),
                  pl.BlockSpec((None, hid, Dh), lambda s, b, g: (s, 0, 0))],
        out_specs=pl.BlockSpec((None, None, None, NS, Dh), lambda s, b, g: (s, b, g, 0, 0)),
        out_shape=jax.ShapeDtypeStruct((2, B, G, NS, Dh), BF16),
        compiler_params=_params("parallel", "parallel", "parallel"),
        name="nsa_compress",
    )(x, cmp_w1, pe, cmp_w2)


def _nsa_kernel(q_ref, kc_ref, vc_ref, ks_ref, vs_ref, kw_ref, vw_ref, zg_ref, bmapt_ref, o_ref,
                bias_sc, m_sc, l_sc, acc_sc, out_sc, *, tq, tk, nsel):
    t0 = pl.program_id(2) * tq
    Dh = NSA_HEAD_DIM
    hpg = NSA_HPG
    nch = tk // LANES
    heads = range(hpg)
    trow = t0 + lax.broadcasted_iota(I32, (tq, 1), 0)
    gates = jax.nn.sigmoid(zg_ref[...])

    def q(h):
        return q_ref[:, h * Dh:(h + 1) * Dh]

    def gate(h, branch):
        return gates[:, 3 * h + branch:3 * h + branch + 1]

    kc, vc = kc_ref[...], vc_ref[...]
    ncb = kc.shape[0]
    blk_end = lax.broadcasted_iota(I32, (1, ncb), 1) * CMP_STRIDE + (CMP_BLOCK - 1)
    valid = blk_end <= trow
    bias_c = jnp.where(valid, 0.0, NEG_INF)
    validf = jnp.where(valid, 1.0, 0.0)
    psum = None
    for h in heads:
        sm = _dot_nt(q(h), kc) + bias_c
        p = jnp.exp(sm - jnp.max(sm, axis=1, keepdims=True)) * validf
        pc = p / jnp.maximum(jnp.sum(p, axis=1, keepdims=True), 1e-30)
        out_sc[h] = gate(h, 0) * _dot(pc.astype(BF16), vc)
        psum = pc if psum is None else psum + pc

    bmt = bmapt_ref[...]
    hi = psum.astype(BF16)
    rest = psum - hi.astype(F32)
    mid = rest.astype(BF16)
    lo = (rest - mid.astype(F32)).astype(BF16)
    imp = _dot_nt(bmt, hi) + _dot_nt(bmt, mid) + _dot_nt(bmt, lo)
    nb = imp.shape[0]
    jb = lax.broadcasted_iota(I32, (nb, tq), 0)
    cur = (t0 + lax.broadcasted_iota(I32, (1, tq), 1)) // SEL_BLOCK
    forced = (jb == 0) | (jb == cur) | (jb == cur - 1)
    imp = jnp.where(forced, FORCE_SCORE, imp)
    imp = jnp.where(jb <= cur, imp, NEG_INF)
    rank = jnp.zeros((nb, tq), F32)
    for i in range(nb):
        ri = imp[i:i + 1, :]
        rank = rank + jnp.where((ri > imp) | ((ri == imp) & (jb > i)), 1.0, 0.0)
    sel_t = jnp.where(rank < nsel, 1.0, 0.0)

    bpt = tk // SEL_BLOCK
    nkt = (t0 + tq + tk - 1) // tk

    def make_bias(kt, carry):
        expand = jnp.where(lax.broadcasted_iota(I32, (nb, tk), 0)
                           == kt * bpt + lax.broadcasted_iota(I32, (nb, tk), 1) // SEL_BLOCK, 1.0, 0.0)
        chosen = _dot_tn(sel_t, expand)
        kpos = kt * tk + lax.broadcasted_iota(I32, (1, tk), 1)
        bias_sc[kt] = jnp.where((chosen > 0.5) & (kpos <= trow), 0.0, NEG_INF)
        return carry

    lax.fori_loop(0, nkt, make_bias, 0)
    m_sc[...] = jnp.full(m_sc.shape, NEG_INF, F32)
    l_sc[...] = jnp.zeros(l_sc.shape, F32)
    acc_sc[...] = jnp.zeros(acc_sc.shape, F32)

    def chunks(a):
        return [a[:, c * LANES:(c + 1) * LANES] for c in range(nch)]

    def max_sweep(kt, carry):
        k = ks_ref[pl.ds(pl.multiple_of(kt * tk, tk), tk), :]
        bias = bias_sc[kt]
        for h in heads:
            m = m_sc[h]
            for sc in chunks(_dot_nt(q(h), k) + bias):
                m = jnp.maximum(m, sc)
            m_sc[h] = m
        return carry

    lax.fori_loop(0, nkt, max_sweep, 0)
    for h in heads:
        m_sc[h] = jnp.broadcast_to(jnp.max(m_sc[h], axis=1, keepdims=True), (tq, LANES))

    def exp_sweep(kt, carry):
        k0 = pl.multiple_of(kt * tk, tk)
        k = ks_ref[pl.ds(k0, tk), :]
        v = vs_ref[pl.ds(k0, tk), :]
        bias = bias_sc[kt]
        for h in heads:
            m = m_sc[h]
            ps = [jnp.exp(sc - m) for sc in chunks(_dot_nt(q(h), k) + bias)]
            lsum = l_sc[h]
            for pchunk in ps:
                lsum = lsum + pchunk
            l_sc[h] = lsum
            acc_sc[h] = acc_sc[h] + _dot(jnp.concatenate(ps, axis=1).astype(BF16), v)
        return carry

    lax.fori_loop(0, nkt, exp_sweep, 0)
    for h in heads:
        out_sc[h] = out_sc[h] + gate(h, 1) * (acc_sc[h] / jnp.sum(l_sc[h], axis=1, keepdims=True))

    span = WINDOW + tq
    w0 = pl.multiple_of(jnp.maximum(t0 - WINDOW, 0), tq)
    kwv = kw_ref[pl.ds(w0, span), :]
    vwv = vw_ref[pl.ds(w0, span), :]
    diff = trow - (w0 + lax.broadcasted_iota(I32, (1, span), 1))
    bias_w = jnp.where((diff >= 0) & (diff < WINDOW), 0.0, NEG_INF)
    for h in heads:
        sw = _dot_nt(q(h), kwv) + bias_w
        pw = jnp.exp(sw - jnp.max(sw, axis=1, keepdims=True))
        o_w = _dot(pw.astype(BF16), vwv) / jnp.sum(pw, axis=1, keepdims=True)
        o_ref[:, h * Dh:(h + 1) * Dh] = (out_sc[h] + gate(h, 2) * o_w).astype(o_ref.dtype)


def block_map(nc, nb):
    i = np.arange(nc)[:, None] * CMP_STRIDE
    j = np.arange(nb)[None, :] * SEL_BLOCK
    inter = np.clip(np.minimum(i + CMP_BLOCK, j + SEL_BLOCK) - np.maximum(i, j), 0, None)
    return jnp.asarray(inter.astype(np.float32) / np.float32(CMP_BLOCK))


def nsa_attention(zqk, zv, zgate, kv_cmp, B, S, qk_col0, v_col0, gate_col0):
    T = B * S
    G, Dh, hpg = NSA_KV_GROUPS, NSA_HEAD_DIM, NSA_HPG
    GW = G * Dh
    tq = 256
    tk = _pick(S, 512)
    assert S % tq == 0 and WINDOW % tq == 0 and S >= WINDOW + tq and tk % SEL_BLOCK == 0
    NS = kv_cmp.shape[3]
    nb = S // SEL_BLOCK
    nq = S // tq
    bmap_t = block_map(NS, nb).T.astype(BF16)
    assert CMP_BLOCK in (32, 64, 128)
    qb0 = qk_col0 // (hpg * Dh)
    ks_b = (qk_col0 + NSA_Q_WIDTH + GW) // Dh
    kw_b = (qk_col0 + NSA_Q_WIDTH + 2 * GW) // Dh
    vs_b = (v_col0 + GW) // Dh
    vw_b = (v_col0 + 2 * GW) // Dh
    g_b = gate_col0 // LANES
    head_f32 = pltpu.VMEM((hpg, tq, Dh), F32)
    cmp_spec = lambda s: pl.BlockSpec((None, None, None, NS, Dh), lambda b, g, i: (s, b, g, 0, 0))
    seq_spec = lambda blk: pl.BlockSpec((S, Dh), lambda b, g, i: (b, blk + g))
    return pl.pallas_call(
        functools.partial(_nsa_kernel, tq=tq, tk=tk, nsel=min(N_SELECT, nb)),
        grid=(B, G, nq),
        in_specs=[pl.BlockSpec((tq, hpg * Dh), lambda b, g, i: (b * nq + i, qb0 + g)),
                  cmp_spec(0), cmp_spec(1),
                  seq_spec(ks_b), seq_spec(vs_b), seq_spec(kw_b), seq_spec(vw_b),
                  pl.BlockSpec((tq, LANES), lambda b, g, i: (b * nq + i, g_b + g)),
                  pl.BlockSpec((nb, NS), lambda b, g, i: (0, 0))],
        out_specs=pl.BlockSpec((tq, hpg * Dh), lambda b, g, i: (b * nq + i, g)),
        out_shape=jax.ShapeDtypeStruct((T, NSA_Q_WIDTH), BF16),
        scratch_shapes=[pltpu.VMEM((S // tk, tq, tk), F32), head_f32, head_f32, head_f32, head_f32],
        compiler_params=_params("parallel", "parallel", "arbitrary"),
        name="nsa_attention",
    )(zqk, kv_cmp, kv_cmp, zqk, zv, zqk, zv, zgate, bmap_t)


def _rwkv_prep_kernel(u_ref, up_ref, mu_ref, w0_ref, a0_ref, wb_ref, ab_ref, gb_ref,
                      r_ref, k_ref, v_ref, ld_ref, a_ref, g_ref, *, tm, S):
    i = pl.program_id(0)
    W = RWKV_WIDTH
    u = u_ref[...]
    prev = jnp.where((i * tm) % S == 0, 0.0, up_ref[...][SUBLANES - 1:SUBLANES])
    row = lax.broadcasted_iota(I32, u.shape, 0)
    u_prev = jnp.where(row == 0, prev, pltpu.roll(u, 1, 0))
    um = u + (u_prev - u) * mu_ref[...]
    r_ref[...] = um[:, :W]
    k_ref[...] = um[:, W:2 * W]
    v_ref[...] = um[:, 2 * W:3 * W]
    lora = um[:, RWKV_TAIL:RWKV_TAIL + LANES]
    gl = um[:, RWKV_TAIL + LANES:RWKV_TAIL + LANES + gb_ref.shape[0]]
    y = w0_ref[...] + _dot(jnp.tanh(lora), wb_ref[...], precision=HI)
    ld_ref[...] = -float(np.exp(-0.5)) * jax.nn.sigmoid(y)
    a_ref[...] = jax.nn.sigmoid(a0_ref[...] + _dot(lora, ab_ref[...], precision=HI))
    g_ref[...] = _dot(jax.nn.sigmoid(gl), gb_ref[...], precision=HI)


def rwkv_prep(zr, rwkv_mu, w0, wb, a0, ab, gb, S):
    T = zr.shape[0]
    W, PW = RWKV_WIDTH, RWKV_PAD_WIDTH
    assert RWKV_DECAY_LORA + RWKV_A_LORA == LANES
    glw = -(-RWKV_GATE_LORA // LANES) * LANES
    assert RWKV_TAIL + LANES + glw <= PW
    tm = _pick(S, 256)
    mu = jnp.zeros((1, PW), F32).at[0, :RWKV_IN_WIDTH].set(rwkv_mu)
    wbp = jnp.zeros((LANES, W), F32).at[:RWKV_DECAY_LORA].set(wb)
    abp = jnp.zeros((LANES, W), F32).at[RWKV_DECAY_LORA:].set(ab)
    gbp = jnp.zeros((glw, W), F32).at[:RWKV_GATE_LORA].set(gb)
    row_spec = pl.BlockSpec((1, W), lambda i: (0, 0))
    out_spec = pl.BlockSpec((tm, W), lambda i: (i, 0))
    return pl.pallas_call(
        functools.partial(_rwkv_prep_kernel, tm=tm, S=S),
        grid=(T // tm,),
        in_specs=[pl.BlockSpec((tm, PW), lambda i: (i, 0)),
                  pl.BlockSpec((SUBLANES, PW), lambda i: (jnp.maximum(i * (tm // SUBLANES) - 1, 0), 0)),
                  pl.BlockSpec((1, PW), lambda i: (0, 0)),
                  row_spec, row_spec,
                  pl.BlockSpec((LANES, W), lambda i: (0, 0)),
                  pl.BlockSpec((LANES, W), lambda i: (0, 0)),
                  pl.BlockSpec((glw, W), lambda i: (0, 0))],
        out_specs=[out_spec] * 6,
        out_shape=[jax.ShapeDtypeStruct((T, W), F32)] * 6,
        compiler_params=_params("parallel"),
        name="rwkv_prep",
    )(zr, zr, mu, w0.reshape(1, W), a0.reshape(1, W), wbp, abp, gbp)


def _rwkv_kernel(r_ref, k_ref, v_ref, ld_ref, a_ref, g_ref, kks_ref, ka_ref, rk_ref, lnw_ref, lnb_ref,
                 o_ref, ht_sc, *, pairs):
    C = RWKV_CHUNK
    N = RWKV_HEAD_DIM
    P2 = 2 * C

    @pl.when(pl.program_id(2) == 0)
    def _():
        ht_sc[...] = jnp.zeros(ht_sc.shape, F32)

    row = lax.broadcasted_iota(I32, (P2, LANES), 0)
    col = lax.broadcasted_iota(I32, (P2, LANES), 1)
    same = (row // C) == (col // N)
    sameq = (row // C) == (col // C)
    incl = sameq & (col <= row)
    strict = sameq & (col < row)
    grp = jnp.where((row // N) == (col // N), 1.0, 0.0).astype(BF16)
    tr = lax.broadcasted_iota(I32, (C, C), 0)
    tc = lax.broadcasted_iota(I32, (C, C), 1)
    lower = jnp.where(tc <= tr, 1.0, 0.0).astype(BF16)

    def split(x):
        hi = x.astype(BF16)
        return hi, (x - hi.astype(F32)).astype(BF16)

    def group_sum(x):
        hi, lo = split(x)
        return _dot(hi, grp) + _dot(lo, grp)

    def stack(x):
        return jnp.where(same, jnp.concatenate([x, x], axis=0), 0.0).astype(BF16)

    P = range(pairs)
    sl = [slice(p * LANES, (p + 1) * LANES) for p in P]

    def each(fn):
        return [fn(p) for p in P]

    r = each(lambda p: r_ref[:, sl[p]])
    kraw = each(lambda p: k_ref[:, sl[p]])
    v = each(lambda p: v_ref[:, sl[p]])
    ld = each(lambda p: ld_ref[:, sl[p]])
    a = each(lambda p: a_ref[:, sl[p]])
    kk = each(lambda p: kraw[p] * kks_ref[:, sl[p]])
    nrm = each(lambda p: group_sum(kk[p] * kk[p]))
    kk = each(lambda p: kk[p] / jnp.maximum(jnp.sqrt(nrm[p]), 1e-12))
    k = each(lambda p: kraw[p] * (1.0 + (a[p] - 1.0) * ka_ref[:, sl[p]]))
    ldp = each(lambda p: split(ld[p]))
    cum = each(lambda p: _dot(lower, ldp[p][0]) + _dot(lower, ldp[p][1]))
    tot = each(lambda p: cum[p][C - 1:C])
    beta = each(lambda p: kk[p] * a[p])
    a_s = each(lambda p: stack(-kk[p] * jnp.exp(cum[p] - ld[p])))
    r_s = each(lambda p: stack(r[p] * jnp.exp(cum[p])))
    e_inv = each(lambda p: jnp.exp(-cum[p]))
    e_rem = each(lambda p: jnp.exp(tot[p] - cum[p]))
    bt_s = each(lambda p: stack(beta[p] * e_inv[p]))
    kt_s = each(lambda p: stack(k[p] * e_inv[p]))
    bh_s = each(lambda p: stack(beta[p] * e_rem[p]))
    kh_s = each(lambda p: stack(k[p] * e_rem[p]))
    v_s = each(lambda p: stack(v[p]))
    quad = each(lambda p: _dot_nt(jnp.concatenate([a_s[p], r_s[p]], axis=0),
                                  jnp.concatenate([bt_s[p], kt_s[p]], axis=0)))
    lp = each(lambda p: jnp.where(strict, quad[p][:P2, :P2], 0.0))
    l_ak = each(lambda p: jnp.where(strict, quad[p][:P2, P2:], 0.0).astype(BF16))
    m_rb = each(lambda p: jnp.where(incl, quad[p][P2:, :P2], 0.0).astype(BF16))
    m_rk = each(lambda p: jnp.where(incl, quad[p][P2:, P2:], 0.0).astype(BF16))
    x = each(lambda p: jnp.concatenate([a_s[p].astype(F32), _dot(l_ak[p], v_s[p])], axis=1))
    steps = int(np.log2(C))
    for s in range(steps):
        lpb = each(lambda p: lp[p].astype(BF16))
        x = each(lambda p: x[p] + _dot(lpb[p], x[p].astype(BF16)))
        if s + 1 < steps:
            lp = each(lambda p: _dot(lpb[p], lpb[p]))
    xb = each(lambda p: x[p].astype(BF16))
    ht = each(lambda p: ht_sc[p])
    htb = each(lambda p: ht[p].astype(BF16))
    qo = each(lambda p: _dot(m_rb[p], xb[p]))
    q_eff = each(lambda p: (r_s[p].astype(F32) + qo[p][:, :LANES]).astype(BF16))
    o = each(lambda p: _dot_nt(q_eff[p], htb[p]) + qo[p][:, LANES:] + _dot(m_rk[p], v_s[p]))
    pgt = each(lambda p: _dot_tn(xb[p], bh_s[p]))
    for p in P:
        ht_sc[p] = (ht[p] * jnp.exp(tot[p]) + _dot(htb[p], pgt[p][:LANES].astype(BF16)) + pgt[p][LANES:]
                    + _dot_tn(v_s[p], kh_s[p]))
    o = each(lambda p: o[p][:C] + o[p][C:])
    mean = each(lambda p: group_sum(o[p]) * (1.0 / N))
    d = each(lambda p: o[p] - mean[p])
    var = each(lambda p: group_sum(d[p] * d[p]) * (1.0 / N))
    bonus = each(lambda p: group_sum(r[p] * k[p] * rk_ref[:, sl[p]]))
    for p in P:
        y = d[p] * lax.rsqrt(var[p] + RWKV_LN_EPS) * lnw_ref[:, sl[p]] + lnb_ref[:, sl[p]]
        o_ref[:, sl[p]] = ((y + bonus[p] * v[p]) * g_ref[:, sl[p]]).astype(o_ref.dtype)


def rwkv_mix(r, k, v, ld, a, g, kk_scale, ka, rk, ln_w, ln_b, B, S):
    T, W = r.shape
    C = RWKV_CHUNK
    assert 2 * RWKV_HEAD_DIM == LANES and 2 * C == LANES and S % C == 0
    pairs = 8
    bw = pairs * LANES
    nc = S // C
    seq = pl.BlockSpec((C, bw), lambda b, p, c: (b * nc + c, p))
    par = pl.BlockSpec((1, bw), lambda b, p, c: (0, p))
    return pl.pallas_call(
        functools.partial(_rwkv_kernel, pairs=pairs),
        grid=(B, W // bw, nc),
        in_specs=[seq] * 6 + [par] * 5,
        out_specs=seq,
        out_shape=jax.ShapeDtypeStruct((T, W), BF16),
        scratch_shapes=[pltpu.VMEM((pairs, LANES, LANES), F32)],
        compiler_params=_params("parallel", "parallel", "arbitrary"),
        name="rwkv_mix",
    )(r, k, v, ld, a, g, *(z.reshape(1, W) for z in (kk_scale, ka, rk, ln_w, ln_b)))


def _merge_kernel(ya_ref, yb_ref, yc_ref, pa_ref, pb_ref, pc_ref, ga_ref, gb_ref, gc_ref, o_ref):
    m = ga_ref[...].astype(F32) * _dot(ya_ref[...], pa_ref[...])
    m = m + gb_ref[...].astype(F32) * _dot(yb_ref[...], pb_ref[...])
    m = m + gc_ref[...].astype(F32) * _dot(yc_ref[...], pc_ref[...])
    o_ref[...] = m.astype(o_ref.dtype)


def branch_merge(ya, yb, yc, pa, pb, pc, gm, D):
    T = ya.shape[0]
    tm, tn = _pick(T, 1024), _pick(D, 512)
    nj = D // tn
    ysp = lambda y: pl.BlockSpec((tm, y.shape[1]), lambda i, j: (i, 0))
    psp = lambda p: pl.BlockSpec((p.shape[0], tn), lambda i, j: (0, j))
    gsp = lambda n: pl.BlockSpec((tm, tn), lambda i, j: (i, n * nj + j))
    return pl.pallas_call(
        _merge_kernel,
        grid=(T // tm, nj),
        in_specs=[ysp(ya), ysp(yb), ysp(yc), psp(pa), psp(pb), psp(pc), gsp(0), gsp(1), gsp(2)],
        out_specs=pl.BlockSpec((tm, tn), lambda i, j: (i, j)),
        out_shape=jax.ShapeDtypeStruct((T, D), BF16),
        compiler_params=_params("parallel", "arbitrary"),
        name="branch_merge",
    )(ya, yb, yc, pa, pb, pc, gm, gm, gm)


MOE_DMA_UNROLL = 32


def _moe_kernel(blk_e_ref, asg_ref, h_hbm, wg_ref, bg_ref, wu_ref, bu_ref, wd_ref, bd_ref, y_hbm,
                xbuf, ybuf, gsem, ssem, *, R, n_blk, A):
    i = pl.program_id(0)
    slot = i % 2
    D = xbuf.shape[-1]

    def per_row(fn):
        def body(r, c):
            fn(r)
            return c
        lax.fori_loop(0, R, body, 0, unroll=MOE_DMA_UNROLL)

    def start_gather(step, s):
        def one(r):
            tok = jnp.minimum(asg_ref[step * R + r] >> TOP_K_SHIFT, A // TOP_K - 1)
            pltpu.make_async_copy(h_hbm.at[pl.ds(tok, 1)], xbuf.at[s, pl.ds(r, 1)], gsem.at[s]).start()
        per_row(one)

    def start_scatter(step, s):
        def one(r):
            dst = asg_ref[step * R + r]
            col = pl.multiple_of((dst & (TOP_K - 1)) * D, LANES)
            pltpu.make_async_copy(ybuf.at[s, pl.ds(r, 1)],
                                  y_hbm.at[pl.ds(dst >> TOP_K_SHIFT, 1), pl.ds(col, D)], ssem.at[s]).start()
        per_row(one)

    def wait_gather(s):
        pltpu.make_async_copy(h_hbm.at[pl.ds(0, R)], xbuf.at[s], gsem.at[s]).wait()

    def wait_scatter(s):
        pltpu.make_async_copy(ybuf.at[s], y_hbm.at[pl.ds(0, R), pl.ds(0, D)], ssem.at[s]).wait()

    @pl.when(i == 0)
    def _():
        start_gather(0, 0)

    @pl.when(i + 1 < n_blk)
    def _():
        start_gather(i + 1, 1 - slot)

    wait_gather(slot)
    x = xbuf[slot].astype(BF16)
    gt = jnp.minimum(_dot(x, wg_ref[...]) + bg_ref[...], SWIGLU_LIMIT)
    up = jnp.clip(_dot(x, wu_ref[...]) + bu_ref[...], -SWIGLU_LIMIT, SWIGLU_LIMIT)
    act = gt * jax.nn.sigmoid(SWIGLU_ALPHA * gt) * (up + 1.0)
    y = _dot(act.astype(BF16), wd_ref[...]) + bd_ref[...]

    @pl.when(i >= 2)
    def _():
        wait_scatter(slot)

    ybuf[slot] = y
    start_scatter(i, slot)

    @pl.when(i == n_blk - 1)
    def _():
        wait_scatter(1 - slot)
        wait_scatter(slot)


def moe_experts(h, blk_e, row_asg, wg, bg, wu, bu, wd, bd, R):
    T, D = h.shape
    E, _, F = wg.shape
    n_blk = blk_e.shape[0]
    A = T * TOP_K
    assert n_blk >= 2 and (n_blk * R) % TOP_K == 0 and 1 << TOP_K_SHIFT == TOP_K
    wspec = lambda shape: pl.BlockSpec((None,) + shape, lambda i, be, ra: (be[i], 0, 0))
    return pl.pallas_call(
        functools.partial(_moe_kernel, R=R, n_blk=n_blk, A=A),
        grid_spec=pltpu.PrefetchScalarGridSpec(
            num_scalar_prefetch=2,
            grid=(n_blk,),
            in_specs=[pl.BlockSpec(memory_space=pl.ANY),
                      wspec((D, F)), wspec((1, F)), wspec((D, F)), wspec((1, F)),
                      wspec((F, D)), wspec((1, D))],
            out_specs=pl.BlockSpec(memory_space=pl.ANY),
            scratch_shapes=[pltpu.VMEM((2, R, D), F32), pltpu.VMEM((2, R, D), F32),
                            pltpu.SemaphoreType.DMA((2,)), pltpu.SemaphoreType.DMA((2,))]),
        out_shape=jax.ShapeDtypeStruct((n_blk * R // TOP_K, TOP_K * D), F32),
        compiler_params=_params("arbitrary"),
        name="moe_experts",
    )(blk_e, row_asg, h, wg, bg.reshape(E, 1, F), wu, bu.reshape(E, 1, F), wd, bd.reshape(E, 1, D))


def _combine_kernel(x_ref, y_ref, wt_ref, gate_ref, o_ref, *, D):
    wt = wt_ref[...]
    acc = wt[:, 0:1] * y_ref[:, 0:D]
    for k in range(1, TOP_K):
        acc = acc + wt[:, k:k + 1] * y_ref[:, k * D:(k + 1) * D]
    o_ref[...] = x_ref[...] + gate_ref[0] * acc


def moe_combine(x, y, wt, gate, S):
    T, D = x.shape
    B = gate.shape[0]
    tm = _pick(S, 128)
    spt = S // tm
    return pl.pallas_call(
        functools.partial(_combine_kernel, D=D),
        grid=(T // tm,),
        in_specs=[pl.BlockSpec((tm, D), lambda i: (i, 0)),
                  pl.BlockSpec((tm, TOP_K * D), lambda i: (i, 0)),
                  pl.BlockSpec((tm, LANES), lambda i: (i, 0)),
                  pl.BlockSpec((1, 1, D), lambda i: (i // spt, 0, 0))],
        out_specs=pl.BlockSpec((tm, D), lambda i: (i, 0)),
        out_shape=jax.ShapeDtypeStruct((T, D), F32),
        compiler_params=_params("parallel"),
        name="moe_combine",
    )(x, y, wt, gate.reshape(B, 1, D))


def route_tables(top_idx, E, R):
    T = top_idx.shape[0]
    A = T * TOP_K
    flat_e = top_idx.reshape(A)
    onehot = (flat_e[:, None] == jnp.arange(E, dtype=I32)[None, :]).astype(I32)
    rank = jnp.take_along_axis(jnp.cumsum(onehot, axis=0) - onehot, flat_e[:, None], axis=1)[:, 0]
    counts = jnp.sum(onehot, axis=0)
    padded = (counts + R - 1) // R * R
    pends = jnp.cumsum(padded)
    dest = (pends - padded)[flat_e] + rank
    n_blk = -(-(A + E * (R - 1)) // R)
    real = jnp.full((n_blk * R,), -1, I32).at[dest].set(jnp.arange(A, dtype=I32))
    pad_id = A - 1 + jnp.cumsum((real < 0).astype(I32))
    row_asg = jnp.where(real < 0, pad_id, real)
    blk_e = jnp.minimum(jnp.searchsorted(pends, jnp.arange(n_blk, dtype=I32) * R, side='right'),
                        E - 1).astype(I32)
    return blk_e, row_asg


def _mixer_weights(w_in, D):
    CW, QW, GW = CONV_WIDTH, NSA_Q_WIDTH, NSA_G_WIDTH
    o_q = 3 * CW
    o_kv = o_q + QW
    o_g = o_kv + 6 * GW
    o_r = o_g + NSA_GATE_WIDTH
    o_m = o_r + RWKV_IN_WIDTH
    kv = lambda j: w_in[:, o_kv + j * GW:o_kv + (j + 1) * GW]
    w_qk = jnp.concatenate([w_in[:, o_q:o_kv], kv(0), kv(2), kv(4)], axis=1).astype(BF16)
    w_cv = jnp.concatenate([w_in[:, :o_q], kv(1), kv(3), kv(5)], axis=1).astype(BF16)
    per_group = NSA_GATE_WIDTH // NSA_KV_GROUPS
    gate_blocks = [jnp.pad(w_in[:, o_g + g * per_group:o_g + (g + 1) * per_group],
                           ((0, 0), (0, LANES - per_group))) for g in range(NSA_KV_GROUPS)]
    w_rg = jnp.concatenate(
        [jnp.pad(w_in[:, o_r:o_m], ((0, 0), (0, RWKV_PAD_WIDTH - RWKV_IN_WIDTH)))] + gate_blocks,
        axis=1).astype(BF16)
    w_m = w_in[:, o_m:].astype(BF16)
    return w_qk, w_cv, w_rg, w_m


def _layer(x, mod, B, S, tables, norm_g, w_in, conv_w, cmp_pe, cmp_w1, cmp_w2, rwkv_mu, rwkv_w0,
           rwkv_wb, rwkv_a0, rwkv_ab, rwkv_gb, rwkv_kk, rwkv_ka, rwkv_rk, rwkv_ln_w, rwkv_ln_b,
           w_branch, w_out, router_w, router_b, exp_wg, exp_bg, exp_wu, exp_bu, exp_wd, exp_bd):
    T, D = x.shape
    CW, QW, GW, Dh, G = CONV_WIDTH, NSA_Q_WIDTH, NSA_G_WIDTH, NSA_HEAD_DIM, NSA_KV_GROUPS
    shift1, scale1, gate1, shift2, scale2, gate2 = (mod[:, n] for n in range(N_MOD))

    h = norm_modulate(x, norm_g[0], scale1, shift1, S, BF16)
    w_qk, w_cv, w_rg, w_m = _mixer_weights(w_in, D)
    zqk = mm_rope(h, w_qk, tables, S)
    zcv = _mm_call(_mm_plain_kernel, h, w_cv, BF16, name="mm_conv_v")
    zrg = _mm_call(_mm_plain_kernel, h, w_rg, F32, name="mm_rwkv_gate")
    gm = _mm_call(_mm_sigmoid_kernel, h, w_m, BF16, name="mm_merge_gate")

    y_a = gated_conv(zcv, conv_w, S)

    ns = S // CMP_STRIDE
    to_strides = lambda z: z.reshape(B, ns, CMP_STRIDE, G, Dh).transpose(0, 3, 1, 2, 4).reshape(
        B, G, ns, CMP_STRIDE * Dh)
    cmp_in = jnp.stack([to_strides(zqk[:, QW:QW + GW]), to_strides(zcv[:, 3 * CW:3 * CW + GW])])
    kv_cmp = compress_kv(cmp_in, cmp_pe, cmp_w1, cmp_w2)
    y_b = nsa_attention(zqk, zcv, zrg, kv_cmp, B, S, qk_col0=0, v_col0=3 * CW, gate_col0=RWKV_PAD_WIDTH)

    r, k, v, ld, a, g = rwkv_prep(zrg, rwkv_mu, rwkv_w0, rwkv_wb, rwkv_a0, rwkv_ab, rwkv_gb, S)
    y_c = rwkv_mix(r, k, v, ld, a, g, rwkv_kk, rwkv_ka, rwkv_rk.reshape(-1), rwkv_ln_w, rwkv_ln_b, B, S)

    wb16 = w_branch.astype(BF16)
    m = branch_merge(y_a, y_b, y_c, wb16[:CW], wb16[CW:CW + QW], wb16[CW + QW:], gm, D)
    x = mm_residual(m, w_out.astype(BF16), x, gate1, S)

    E = router_w.shape[1]
    h2, top_idx, top_w = norm_router(x, norm_g[1], scale2, shift2, router_w, router_b, S)
    R = _pick(T * TOP_K // E, 256)
    blk_e, row_asg = route_tables(top_idx[:, :TOP_K], E, R)
    y = moe_experts(h2, blk_e, row_asg, exp_wg.astype(BF16), exp_bg, exp_wu.astype(BF16), exp_bu,
                    exp_wd.astype(BF16), exp_bd, R)
    return moe_combine(x, y, top_w, gate2, S)


def kernel(x, c, ada_w, ada_b, ada_table, norm_g, final_g, w_in, conv_w, cmp_pe, cmp_w1, cmp_w2, rwkv_mu, rwkv_w0, rwkv_wb, rwkv_a0, rwkv_ab, rwkv_gb, rwkv_kk, rwkv_ka, rwkv_rk, rwkv_ln_w, rwkv_ln_b, w_branch, w_out, router_w, router_b, exp_wg, exp_bg, exp_wu, exp_bu, exp_wd, exp_bd):
    B, S, D = x.shape
    depth = w_in.shape[0]
    tables = rope_lane_tables(S)
    mod_all = ada_modulation(c, ada_w, ada_b).reshape(B, N_MOD, D)
    xf = x.reshape(B * S, D)
    for l in range(depth):
        xf = _layer(xf, mod_all + ada_table[l], B, S, tables, norm_g[l], w_in[l], conv_w[l], cmp_pe[l],
                    cmp_w1[l], cmp_w2[l], rwkv_mu[l], rwkv_w0[l], rwkv_wb[l], rwkv_a0[l], rwkv_ab[l],
                    rwkv_gb[l], rwkv_kk[l], rwkv_ka[l], rwkv_rk[l], rwkv_ln_w[l], rwkv_ln_b[l],
                    w_branch[l], w_out[l], router_w[l], router_b[l], exp_wg[l], exp_bg[l], exp_wu[l],
                    exp_bu[l], exp_wd[l], exp_bd[l])
    return final_norm(xf, final_g).reshape(B, S, D)
```

```python
import functools

import numpy as np
import jax
import jax.numpy as jnp
from jax import lax
from jax.experimental import pallas as pl
from jax.experimental.pallas import tpu as pltpu

F32 = jnp.float32
BF16 = jnp.bfloat16
I32 = jnp.int32
HI = lax.Precision.HIGHEST

N_MOD = 6
RMS_EPS = 1e-6
CONV_WIDTH = 1024
CONV_K = 3
NSA_HEADS = 16
NSA_KV_GROUPS = 4
NSA_HPG = NSA_HEADS // NSA_KV_GROUPS
NSA_HEAD_DIM = 128
CMP_BLOCK = 32
CMP_STRIDE = 16
SEL_BLOCK = 64
N_SELECT = 16
WINDOW = 512
FORCE_SCORE = 1e9
NEG_INF = -1e30
ROPE_THETA = 500000.0
ROPE_DIMS = NSA_HEAD_DIM // 4
NSA_Q_WIDTH = NSA_HEADS * NSA_HEAD_DIM
NSA_G_WIDTH = NSA_KV_GROUPS * NSA_HEAD_DIM
NSA_GATE_WIDTH = 3 * NSA_HEADS
RWKV_HEADS = 16
RWKV_HEAD_DIM = 64
RWKV_WIDTH = RWKV_HEADS * RWKV_HEAD_DIM
RWKV_DECAY_LORA = 64
RWKV_A_LORA = 64
RWKV_GATE_LORA = 160
RWKV_IN_WIDTH = 3 * RWKV_WIDTH + RWKV_DECAY_LORA + RWKV_A_LORA + RWKV_GATE_LORA
RWKV_LN_EPS = 64e-5
RWKV_CHUNK = 64
TOP_K = 4
TOP_K_SHIFT = 2
SWIGLU_ALPHA = 1.702
SWIGLU_LIMIT = 7.0

LANES = 128
SUBLANES = 8
VMEM_BYTES_V7X = 64 * 1024 * 1024
VMEM_LIMIT = VMEM_BYTES_V7X - 12 * 1024 * 1024

RWKV_PAD_WIDTH = -(-RWKV_IN_WIDTH // 512) * 512
RWKV_TAIL = 3 * RWKV_WIDTH


def _params(*sem):
    return pltpu.CompilerParams(dimension_semantics=sem, vmem_limit_bytes=VMEM_LIMIT)


def _pick(n, pref):
    t = min(n, pref)
    while n % t:
        t //= 2
    return t


def _dot(a, b, **kw):
    return jnp.dot(a, b, preferred_element_type=F32, **kw)


def _dot_nt(a, b, **kw):
    return lax.dot_general(a, b, (((1,), (1,)), ((), ())), preferred_element_type=F32, **kw)


def _dot_tn(a, b, **kw):
    return lax.dot_general(a, b, (((0,), (0,)), ((), ())), preferred_element_type=F32, **kw)


def _ada_kernel(c_ref, w_ref, b_ref, o_ref):
    c = c_ref[...]
    sc = c * jax.nn.sigmoid(c)
    o_ref[...] = _dot(sc, w_ref[...], precision=HI) + b_ref[...]


def ada_modulation(c, ada_w, ada_b):
    B, D = c.shape
    N = ada_w.shape[1]
    rows = -(-B // SUBLANES) * SUBLANES
    cp = jnp.zeros((rows, D), F32).at[:B].set(c)
    tn = _pick(N, 512)
    out = pl.pallas_call(
        _ada_kernel,
        grid=(N // tn,),
        in_specs=[pl.BlockSpec((rows, D), lambda j: (0, 0)),
                  pl.BlockSpec((D, tn), lambda j: (0, j)),
                  pl.BlockSpec((1, tn), lambda j: (0, j))],
        out_specs=pl.BlockSpec((rows, tn), lambda j: (0, j)),
        out_shape=jax.ShapeDtypeStruct((rows, N), F32),
        compiler_params=_params("parallel"),
        name="ada_mod",
    )(cp, ada_w, ada_b.reshape(1, N))
    return out[:B]


def _norm_mod(x, g, sc, sh):
    ms = jnp.mean(x * x, axis=-1, keepdims=True)
    y = x * lax.rsqrt(ms + RMS_EPS) * g
    return y * (1.0 + sc) + sh


def _norm_mod_kernel(x_ref, g_ref, sc_ref, sh_ref, o_ref):
    h = _norm_mod(x_ref[...], g_ref[...], sc_ref[0], sh_ref[0])
    o_ref[...] = h.astype(o_ref.dtype)


def norm_modulate(x, g, scale, shift, S, out_dtype):
    T, D = x.shape
    B = scale.shape[0]
    tm = _pick(S, 256)
    spt = S // tm
    return pl.pallas_call(
        _norm_mod_kernel,
        grid=(T // tm,),
        in_specs=[pl.BlockSpec((tm, D), lambda i: (i, 0)),
                  pl.BlockSpec((1, D), lambda i: (0, 0)),
                  pl.BlockSpec((1, 1, D), lambda i: (i // spt, 0, 0)),
                  pl.BlockSpec((1, 1, D), lambda i: (i // spt, 0, 0))],
        out_specs=pl.BlockSpec((tm, D), lambda i: (i, 0)),
        out_shape=jax.ShapeDtypeStruct((T, D), out_dtype),
        compiler_params=_params("parallel"),
        name="norm_mod",
    )(x, g.reshape(1, D), scale.reshape(B, 1, D), shift.reshape(B, 1, D))


def _norm_router_kernel(x_ref, g_ref, sc_ref, sh_ref, rw_ref, rb_ref, h_ref, idx_ref, wt_ref, cnt_ref,
                        base_sc):
    @pl.when(pl.program_id(0) == 0)
    def _():
        base_sc[...] = jnp.zeros(base_sc.shape, F32)

    h = _norm_mod(x_ref[...], g_ref[...], sc_ref[0], sh_ref[0])
    h_ref[...] = h
    logits = _dot(h, rw_ref[...], precision=HI) + rb_ref[...]
    tm = logits.shape[0]
    lane = lax.broadcasted_iota(I32, logits.shape, 1)
    lanef = lane.astype(F32)
    idx_out = jnp.zeros(logits.shape, I32)
    val_out = jnp.zeros(logits.shape, F32)
    v0 = None
    hits = []
    for k in range(TOP_K):
        mx = jnp.max(logits, axis=1, keepdims=True)
        first = jnp.min(jnp.where(logits == mx, lanef, float(LANES)), axis=1, keepdims=True)
        hit = lanef == first
        hits.append(jnp.where(hit, 1.0, 0.0))
        if k == 0:
            v0 = mx
        idx_out = jnp.where(lane == k, first.astype(I32), idx_out)
        val_out = jnp.where(lane == k, jnp.exp(mx - v0), val_out)
        logits = jnp.where(hit, -3e38, logits)
    wt_ref[...] = val_out / jnp.sum(val_out, axis=1, keepdims=True)
    used = hits[0]
    for k in range(1, TOP_K):
        used = used + hits[k]
    earlier = jnp.where(lax.broadcasted_iota(I32, (tm, tm), 1) < lax.broadcasted_iota(I32, (tm, tm), 0),
                        1.0, 0.0).astype(BF16)
    before = base_sc[0:1] + _dot(earlier, used.astype(BF16))
    for k in range(TOP_K):
        rank = jnp.sum(hits[k] * before, axis=1, keepdims=True)
        idx_out = jnp.where(lane == TOP_K + k, rank.astype(I32), idx_out)
    idx_ref[...] = idx_out
    total = base_sc[0:1] + jnp.sum(used, axis=0, keepdims=True)
    base_sc[...] = jnp.broadcast_to(total, base_sc.shape)
    cnt_ref[...] = jnp.broadcast_to(total, cnt_ref.shape).astype(I32)


def norm_router(x, g, scale, shift, router_w, router_b, S):
    T, D = x.shape
    B = scale.shape[0]
    E = router_w.shape[1]
    assert E <= LANES
    rw = jnp.zeros((D, LANES), F32).at[:, :E].set(router_w)
    rb = jnp.full((1, LANES), NEG_INF, F32).at[0, :E].set(router_b)
    tm = _pick(S, 256)
    spt = S // tm
    return pl.pallas_call(
        _norm_router_kernel,
        grid=(T // tm,),
        in_specs=[pl.BlockSpec((tm, D), lambda i: (i, 0)),
                  pl.BlockSpec((1, D), lambda i: (0, 0)),
                  pl.BlockSpec((1, 1, D), lambda i: (i // spt, 0, 0)),
                  pl.BlockSpec((1, 1, D), lambda i: (i // spt, 0, 0)),
                  pl.BlockSpec((D, LANES), lambda i: (0, 0)),
                  pl.BlockSpec((1, LANES), lambda i: (0, 0))],
        out_specs=[pl.BlockSpec((tm, D), lambda i: (i, 0)),
                   pl.BlockSpec((tm, LANES), lambda i: (i, 0)),
                   pl.BlockSpec((tm, LANES), lambda i: (i, 0)),
                   pl.BlockSpec((SUBLANES, LANES), lambda i: (0, 0))],
        out_shape=[jax.ShapeDtypeStruct((T, D), F32),
                   jax.ShapeDtypeStruct((T, LANES), I32),
                   jax.ShapeDtypeStruct((T, LANES), F32),
                   jax.ShapeDtypeStruct((SUBLANES, LANES), I32)],
        scratch_shapes=[pltpu.VMEM((SUBLANES, LANES), F32)],
        compiler_params=_params("arbitrary"),
        name="norm_router",
    )(x, g.reshape(1, D), scale.reshape(B, 1, D), shift.reshape(B, 1, D), rw, rb)


def _rope_lanes(x, cf, sa, sb):
    half = ROPE_DIMS // 2
    return x * cf + pltpu.roll(x, LANES - half, 1) * sa + pltpu.roll(x, half, 1) * sb


def _mm_plain_kernel(a_ref, b_ref, o_ref):
    o_ref[...] = _dot(a_ref[...], b_ref[...]).astype(o_ref.dtype)


def _mm_sigmoid_kernel(a_ref, b_ref, o_ref):
    o_ref[...] = jax.nn.sigmoid(_dot(a_ref[...], b_ref[...])).astype(o_ref.dtype)


def _mm_qkv_kernel(a_ref, b_ref, cf_ref, sa_ref, sb_ref, o_ref, *, q_lo, q_hi, kv_lo):
    j = pl.program_id(1)
    is_q = (j >= q_lo) & (j < q_hi)
    is_key = (j >= kv_lo) & ((j - kv_lo) % 2 == 0)
    acc = _dot(a_ref[...], b_ref[...])

    @pl.when(is_q | is_key)
    def _():
        fac = jnp.where(is_q, NSA_HEAD_DIM ** -0.5, 1.0)
        cf, sa, sb = cf_ref[...], sa_ref[...], sb_ref[...]
        for n in range(acc.shape[1] // LANES):
            sl = slice(n * LANES, (n + 1) * LANES)
            o_ref[:, sl] = _rope_lanes(acc[:, sl] * fac, cf, sa, sb).astype(o_ref.dtype)

    @pl.when(jnp.logical_not(is_q | is_key))
    def _():
        o_ref[...] = acc.astype(o_ref.dtype)


def _mm_residual_kernel(a_ref, b_ref, x_ref, gate_ref, o_ref):
    o_ref[...] = x_ref[...] + gate_ref[0] * _dot(a_ref[...], b_ref[...])


def _mm_call(kernel, a, b, out_dtype, extra=(), extra_specs=(), tm_pref=1024, tn_pref=512, name="mm"):
    M, K = a.shape
    N = b.shape[1]
    tm, tn = _pick(M, tm_pref), _pick(N, tn_pref)
    return pl.pallas_call(
        kernel,
        grid=(M // tm, N // tn),
        in_specs=[pl.BlockSpec((tm, K), lambda i, j: (i, 0)),
                  pl.BlockSpec((K, tn), lambda i, j: (0, j))] + [s(tm, tn) for s in extra_specs],
        out_specs=pl.BlockSpec((tm, tn), lambda i, j: (i, j)),
        out_shape=jax.ShapeDtypeStruct((M, N), out_dtype),
        compiler_params=_params("parallel", "arbitrary"),
        name=name,
    )(a, b, *extra)


def rope_lane_tables(S):
    half = ROPE_DIMS // 2
    inv = ROPE_THETA ** (-jnp.arange(0, ROPE_DIMS, 2, dtype=F32) / ROPE_DIMS)
    ang = jnp.arange(S, dtype=F32)[:, None] * inv[None, :]
    cos, sin = jnp.cos(ang), jnp.sin(ang)
    zeros = jnp.zeros((S, LANES - ROPE_DIMS), F32)
    zh = jnp.zeros((S, half), F32)
    cf = jnp.concatenate([cos, cos, zeros + 1.0], axis=1)
    sa = jnp.concatenate([-sin, zh, zeros], axis=1)
    sb = jnp.concatenate([zh, sin, zeros], axis=1)
    return cf, sa, sb


def mm_qkv(a, b, tables, S):
    T = a.shape[0]
    tm = _pick(min(T, S), 1024)
    spt = S // tm
    spec = lambda tm_, tn_: pl.BlockSpec((tm_, LANES), lambda i, j: (i % spt, 0))
    tn = NSA_G_WIDTH
    q0, kv0 = 3 * CONV_WIDTH, 3 * CONV_WIDTH + NSA_Q_WIDTH
    assert q0 % tn == 0 and kv0 % tn == 0 and b.shape[1] == kv0 + 6 * tn
    kern = functools.partial(_mm_qkv_kernel, q_lo=q0 // tn, q_hi=kv0 // tn, kv_lo=kv0 // tn)
    return _mm_call(kern, a, b, BF16, extra=tables, extra_specs=(spec,) * 3, tm_pref=tm, tn_pref=tn,
                    name="mm_qkv")


def mm_residual(a, b, x, gate, S):
    T = a.shape[0]
    B, D = gate.shape
    tm = _pick(min(T, S), 1024)
    spt = S // tm
    xspec = lambda tm_, tn_: pl.BlockSpec((tm_, tn_), lambda i, j: (i, j))
    gspec = lambda tm_, tn_: pl.BlockSpec((1, 1, tn_), lambda i, j: (i // spt, 0, j))
    return _mm_call(_mm_residual_kernel, a, b, F32, extra=(x, gate.reshape(B, 1, D)),
                    extra_specs=(xspec, gspec), tm_pref=tm, name="mm_residual")


def _conv_kernel(z_ref, zp_ref, w_ref, o_ref, *, tm, S):
    i = pl.program_id(0)
    W = CONV_WIDTH
    z = z_ref[...].astype(F32)
    cb, u = z[:, :W], z[:, W:2 * W] * z[:, 2 * W:]
    zp = zp_ref[...].astype(F32)
    up = zp[:, W:2 * W] * zp[:, 2 * W:]
    up = jnp.where((i * tm) % S == 0, 0.0, up)
    row = lax.broadcasted_iota(I32, u.shape, 0)
    last, prev = up[SUBLANES - 1:SUBLANES], up[SUBLANES - 2:SUBLANES - 1]
    u1 = jnp.where(row == 0, last, pltpu.roll(u, 1, 0))
    u2 = jnp.where(row == 0, prev, jnp.where(row == 1, last, pltpu.roll(u, 2, 0)))
    w = w_ref[...]
    o_ref[...] = (cb * (w[0:1] * u2 + w[1:2] * u1 + w[2:3] * u)).astype(o_ref.dtype)


def gated_conv(z, conv_w, S):
    T = z.shape[0]
    W = CONV_WIDTH
    assert CONV_K == 3
    tm = _pick(S, 512)
    wp = jnp.zeros((SUBLANES, W), F32).at[:CONV_K].set(conv_w)
    return pl.pallas_call(
        functools.partial(_conv_kernel, tm=tm, S=S),
        grid=(T // tm,),
        in_specs=[pl.BlockSpec((tm, 3 * W), lambda i: (i, 0)),
                  pl.BlockSpec((SUBLANES, 3 * W), lambda i: (jnp.maximum(i * (tm // SUBLANES) - 1, 0), 0)),
                  pl.BlockSpec((SUBLANES, W), lambda i: (0, 0))],
        out_specs=pl.BlockSpec((tm, W), lambda i: (i, 0)),
        out_shape=jax.ShapeDtypeStruct((T, W), BF16),
        compiler_params=_params("parallel"),
        name="gated_conv",
    )(z, z, wp)


def _compress_kernel(x_ref, w1_ref, pe_ref, w2_ref, o_ref):
    x = x_ref[...]
    w1 = w1_ref[...]
    half = x.shape[1]
    ns = x.shape[0]
    first = _dot(x, w1[:half].astype(BF16))
    second = _dot(x, w1[half:].astype(BF16))
    bias = _dot(pe_ref[...], w1, precision=HI)[0:1]
    hid = first + pltpu.roll(second, ns - 1, 0) + bias
    act = jax.nn.gelu(hid, approximate=True)
    o_ref[...] = _dot(act.astype(BF16), w2_ref[...].astype(BF16)).astype(o_ref.dtype)


def compress_kv(x, cmp_pe, cmp_w1, cmp_w2):
    _, B, G, NS, W = x.shape
    assert CMP_BLOCK == 2 * CMP_STRIDE
    Dh = NSA_HEAD_DIM
    hid = cmp_w1.shape[-1]
    pe = jnp.broadcast_to(cmp_pe.reshape(2, 1, CMP_BLOCK * Dh), (2, SUBLANES, CMP_BLOCK * Dh))
    return pl.pallas_call(
        _compress_kernel,
        grid=(2, B, G),
        in_specs=[pl.BlockSpec((None, None, None, NS, W), lambda s, b, g: (s, b, g, 0, 0)),
                  pl.BlockSpec((None, CMP_BLOCK * Dh, hid), lambda s, b, g: (s, 0, 0)),
                  pl.BlockSpec((None, SUBLANES, CMP_BLOCK * Dh), lambda s, b, g: (s, 0, 0)),
                  pl.BlockSpec((None, hid, Dh), lambda s, b, g: (s, 0, 0))],
        out_specs=pl.BlockSpec((None, None, None, NS, Dh), lambda s, b, g: (s, b, g, 0, 0)),
        out_shape=jax.ShapeDtypeStruct((2, B, G, NS, Dh), BF16),
        compiler_params=_params("parallel", "parallel", "parallel"),
        name="nsa_compress",
    )(x, cmp_w1, pe, cmp_w2)


def _nsa_kernel(q_ref, kc_ref, vc_ref, ks_ref, vs_ref, kw_ref, vw_ref, zg_ref, bmapt_ref, o_ref,
                s_sc, m_sc, acc_sc, out_sc, *, tq, tk, nsel):
    t0 = pl.program_id(2) * tq
    Dh = NSA_HEAD_DIM
    hpg = NSA_HPG
    nch = tk // LANES
    heads = range(hpg)
    trow = t0 + lax.broadcasted_iota(I32, (tq, 1), 0)
    gates = jax.nn.sigmoid(zg_ref[...])

    def q(h):
        return q_ref[:, h * Dh:(h + 1) * Dh]

    def gate(h, branch):
        return gates[:, 3 * h + branch:3 * h + branch + 1]

    kc, vc = kc_ref[...], vc_ref[...]
    ncb = kc.shape[0]
    blk_end = lax.broadcasted_iota(I32, (1, ncb), 1) * CMP_STRIDE + (CMP_BLOCK - 1)
    valid = blk_end <= trow
    bias_c = jnp.where(valid, 0.0, NEG_INF)
    validf = jnp.where(valid, 1.0, 0.0)
    psum = None
    for h in heads:
        sm = _dot_nt(q(h), kc) + bias_c
        p = jnp.exp(sm - jnp.max(sm, axis=1, keepdims=True)) * validf
        pc = p / jnp.maximum(jnp.sum(p, axis=1, keepdims=True), 1e-30)
        out_sc[h] = gate(h, 0) * _dot(pc.astype(BF16), vc)
        psum = pc if psum is None else psum + pc

    bmt = bmapt_ref[...]
    hi = psum.astype(BF16)
    rest = psum - hi.astype(F32)
    mid = rest.astype(BF16)
    lo = (rest - mid.astype(F32)).astype(BF16)
    imp = _dot_nt(bmt, hi) + _dot_nt(bmt, mid) + _dot_nt(bmt, lo)
    nb = imp.shape[0]
    jb = lax.broadcasted_iota(I32, (nb, tq), 0)
    cur = (t0 + lax.broadcasted_iota(I32, (1, tq), 1)) // SEL_BLOCK
    forced = (jb == 0) | (jb == cur) | (jb == cur - 1)
    imp = jnp.where(forced, FORCE_SCORE, imp)
    imp = jnp.where(jb <= cur, imp, NEG_INF)
    rank = jnp.zeros((nb, tq), F32)
    for i in range(nb):
        ri = imp[i:i + 1, :]
        rank = rank + jnp.where((ri > imp) | ((ri == imp) & (jb > i)), 1.0, 0.0)
    sel_t = jnp.where(rank < nsel, 1.0, 0.0)

    bpt = tk // SEL_BLOCK
    nkt = (t0 + tq + tk - 1) // tk
    m_sc[...] = jnp.full(m_sc.shape, NEG_INF, F32)
    acc_sc[...] = jnp.zeros(acc_sc.shape, F32)

    def chunks(a):
        return [a[:, c * LANES:(c + 1) * LANES] for c in range(nch)]

    def score_sweep(kt, carry):
        k = ks_ref[pl.ds(pl.multiple_of(kt * tk, tk), tk), :]
        expand = jnp.where(lax.broadcasted_iota(I32, (nb, tk), 0)
                           == kt * bpt + lax.broadcasted_iota(I32, (nb, tk), 1) // SEL_BLOCK, 1.0, 0.0)
        chosen = _dot_tn(sel_t, expand)
        kpos = kt * tk + lax.broadcasted_iota(I32, (1, tk), 1)
        bias = jnp.where((chosen > 0.5) & (kpos <= trow), 0.0, NEG_INF)
        for h in heads:
            s = _dot_nt(q(h), k) + bias
            s_sc[kt, h] = s
            m = m_sc[h]
            for sc in chunks(s):
                m = jnp.maximum(m, sc)
            m_sc[h] = m
        return carry

    lax.fori_loop(0, nkt, score_sweep, 0)
    for h in heads:
        m_sc[h] = jnp.broadcast_to(jnp.max(m_sc[h], axis=1, keepdims=True), (tq, LANES))
    ones = jnp.ones((tk, LANES), BF16)

    def exp_sweep(kt, carry):
        v1 = jnp.concatenate([vs_ref[pl.ds(pl.multiple_of(kt * tk, tk), tk), :], ones], axis=1)
        for h in heads:
            m = m_sc[h]
            p = jnp.concatenate([jnp.exp(sc - m) for sc in chunks(s_sc[kt, h])], axis=1)
            acc_sc[h] = acc_sc[h] + _dot(p.astype(BF16), v1)
        return carry

    lax.fori_loop(0, nkt, exp_sweep, 0)
    for h in heads:
        acc = acc_sc[h]
        out_sc[h] = out_sc[h] + gate(h, 1) * (acc[:, :Dh] / acc[:, Dh:Dh + 1])

    span = WINDOW + tq
    w0 = pl.multiple_of(jnp.maximum(t0 - WINDOW, 0), tq)
    kwv = kw_ref[pl.ds(w0, span), :]
    vwv = vw_ref[pl.ds(w0, span), :]
    diff = trow - (w0 + lax.broadcasted_iota(I32, (1, span), 1))
    bias_w = jnp.where((diff >= 0) & (diff < WINDOW), 0.0, NEG_INF)
    for h in heads:
        sw = _dot_nt(q(h), kwv) + bias_w
        pw = jnp.exp(sw - jnp.max(sw, axis=1, keepdims=True))
        o_w = _dot(pw.astype(BF16), vwv) / jnp.sum(pw, axis=1, keepdims=True)
        o_ref[:, h * Dh:(h + 1) * Dh] = (out_sc[h] + gate(h, 2) * o_w).astype(o_ref.dtype)


def block_map(nc, nb):
    i = np.arange(nc)[:, None] * CMP_STRIDE
    j = np.arange(nb)[None, :] * SEL_BLOCK
    inter = np.clip(np.minimum(i + CMP_BLOCK, j + SEL_BLOCK) - np.maximum(i, j), 0, None)
    return jnp.asarray(inter.astype(np.float32) / np.float32(CMP_BLOCK))


def nsa_attention(z, zgate, kv_cmp, B, S, q_col, ks_col, vs_col, kw_col, vw_col, gate_col):
    T = B * S
    G, Dh, hpg = NSA_KV_GROUPS, NSA_HEAD_DIM, NSA_HPG
    GW = G * Dh
    tq = 256
    tk = _pick(S, 512)
    assert S % tq == 0 and WINDOW % tq == 0 and S >= WINDOW + tq and tk % SEL_BLOCK == 0
    NS = kv_cmp.shape[3]
    nb = S // SEL_BLOCK
    nq = S // tq
    bmap_t = block_map(NS, nb).T.astype(BF16)
    assert CMP_BLOCK in (32, 64, 128)
    assert q_col % (hpg * Dh) == 0 and all(c % Dh == 0 for c in (ks_col, vs_col, kw_col, vw_col))
    qb0 = q_col // (hpg * Dh)
    ks_b, vs_b, kw_b, vw_b = (c // Dh for c in (ks_col, vs_col, kw_col, vw_col))
    g_b = gate_col // LANES
    head_f32 = pltpu.VMEM((hpg, tq, Dh), F32)
    cmp_spec = lambda s: pl.BlockSpec((None, None, None, NS, Dh), lambda b, g, i: (s, b, g, 0, 0))
    seq_spec = lambda blk: pl.BlockSpec((S, Dh), lambda b, g, i: (b, blk + g))
    return pl.pallas_call(
        functools.partial(_nsa_kernel, tq=tq, tk=tk, nsel=min(N_SELECT, nb)),
        grid=(B, G, nq),
        in_specs=[pl.BlockSpec((tq, hpg * Dh), lambda b, g, i: (b * nq + i, qb0 + g)),
                  cmp_spec(0), cmp_spec(1),
                  seq_spec(ks_b), seq_spec(vs_b), seq_spec(kw_b), seq_spec(vw_b),
                  pl.BlockSpec((tq, LANES), lambda b, g, i: (b * nq + i, g_b + g)),
                  pl.BlockSpec((nb, NS), lambda b, g, i: (0, 0))],
        out_specs=pl.BlockSpec((tq, hpg * Dh), lambda b, g, i: (b * nq + i, g)),
        out_shape=jax.ShapeDtypeStruct((T, NSA_Q_WIDTH), BF16),
        scratch_shapes=[pltpu.VMEM((S // tk, hpg, tq, tk), F32), head_f32,
                        pltpu.VMEM((hpg, tq, 2 * Dh), F32), head_f32],
        compiler_params=_params("parallel", "parallel", "arbitrary"),
        name="nsa_attention",
    )(z, kv_cmp, kv_cmp, z, z, z, z, zgate, bmap_t)


def _rwkv_prep_kernel(u_ref, up_ref, mu_ref, w0_ref, a0_ref, wb_ref, ab_ref, gb_ref,
                      r_ref, k_ref, v_ref, ld_ref, a_ref, g_ref, *, tm, S):
    i = pl.program_id(0)
    W = RWKV_WIDTH
    u = u_ref[...]
    prev = jnp.where((i * tm) % S == 0, 0.0, up_ref[...][SUBLANES - 1:SUBLANES])
    row = lax.broadcasted_iota(I32, u.shape, 0)
    u_prev = jnp.where(row == 0, prev, pltpu.roll(u, 1, 0))
    um = u + (u_prev - u) * mu_ref[...]
    r_ref[...] = um[:, :W]
    k_ref[...] = um[:, W:2 * W]
    v_ref[...] = um[:, 2 * W:3 * W]
    lora = um[:, RWKV_TAIL:RWKV_TAIL + LANES]
    gl = um[:, RWKV_TAIL + LANES:RWKV_TAIL + LANES + gb_ref.shape[0]]
    y = w0_ref[...] + _dot(jnp.tanh(lora), wb_ref[...], precision=HI)
    ld_ref[...] = -float(np.exp(-0.5)) * jax.nn.sigmoid(y)
    a_ref[...] = jax.nn.sigmoid(a0_ref[...] + _dot(lora, ab_ref[...], precision=HI))
    g_ref[...] = _dot(jax.nn.sigmoid(gl), gb_ref[...], precision=HI)


def rwkv_prep(zr, rwkv_mu, w0, wb, a0, ab, gb, S):
    T = zr.shape[0]
    W, PW = RWKV_WIDTH, RWKV_PAD_WIDTH
    assert RWKV_DECAY_LORA + RWKV_A_LORA == LANES
    glw = -(-RWKV_GATE_LORA // LANES) * LANES
    assert RWKV_TAIL + LANES + glw <= PW
    tm = _pick(S, 256)
    mu = jnp.zeros((1, PW), F32).at[0, :RWKV_IN_WIDTH].set(rwkv_mu)
    wbp = jnp.zeros((LANES, W), F32).at[:RWKV_DECAY_LORA].set(wb)
    abp = jnp.zeros((LANES, W), F32).at[RWKV_DECAY_LORA:].set(ab)
    gbp = jnp.zeros((glw, W), F32).at[:RWKV_GATE_LORA].set(gb)
    row_spec = pl.BlockSpec((1, W), lambda i: (0, 0))
    out_spec = pl.BlockSpec((tm, W), lambda i: (i, 0))
    return pl.pallas_call(
        functools.partial(_rwkv_prep_kernel, tm=tm, S=S),
        grid=(T // tm,),
        in_specs=[pl.BlockSpec((tm, PW), lambda i: (i, 0)),
                  pl.BlockSpec((SUBLANES, PW), lambda i: (jnp.maximum(i * (tm // SUBLANES) - 1, 0), 0)),
                  pl.BlockSpec((1, PW), lambda i: (0, 0)),
                  row_spec, row_spec,
                  pl.BlockSpec((LANES, W), lambda i: (0, 0)),
                  pl.BlockSpec((LANES, W), lambda i: (0, 0)),
                  pl.BlockSpec((glw, W), lambda i: (0, 0))],
        out_specs=[out_spec] * 6,
        out_shape=[jax.ShapeDtypeStruct((T, W), F32)] * 6,
        compiler_params=_params("parallel"),
        name="rwkv_prep",
    )(zr, zr, mu, w0.reshape(1, W), a0.reshape(1, W), wbp, abp, gbp)


def _rwkv_kernel(r_ref, k_ref, v_ref, ld_ref, a_ref, g_ref, kks_ref, ka_ref, rk_ref, lnw_ref, lnb_ref,
                 o_ref, ht_sc, *, pairs):
    C = RWKV_CHUNK
    N = RWKV_HEAD_DIM
    P2 = 2 * C

    @pl.when(pl.program_id(2) == 0)
    def _():
        ht_sc[...] = jnp.zeros(ht_sc.shape, F32)

    row = lax.broadcasted_iota(I32, (P2, LANES), 0)
    col = lax.broadcasted_iota(I32, (P2, LANES), 1)
    same = (row // C) == (col // N)
    sameq = (row // C) == (col // C)
    incl = sameq & (col <= row)
    strict = sameq & (col < row)
    grp = jnp.where((row // N) == (col // N), 1.0, 0.0).astype(BF16)
    tr = lax.broadcasted_iota(I32, (C, C), 0)
    tc = lax.broadcasted_iota(I32, (C, C), 1)
    lower = jnp.where(tc <= tr, 1.0, 0.0).astype(BF16)

    def split(x):
        hi = x.astype(BF16)
        return hi, (x - hi.astype(F32)).astype(BF16)

    def group_sum(x):
        hi, lo = split(x)
        return _dot(hi, grp) + _dot(lo, grp)

    def stack(x):
        return jnp.where(same, jnp.concatenate([x, x], axis=0), 0.0).astype(BF16)

    P = range(pairs)
    sl = [slice(p * LANES, (p + 1) * LANES) for p in P]

    def each(fn):
        return [fn(p) for p in P]

    r = each(lambda p: r_ref[:, sl[p]])
    kraw = each(lambda p: k_ref[:, sl[p]])
    v = each(lambda p: v_ref[:, sl[p]])
    ld = each(lambda p: ld_ref[:, sl[p]])
    a = each(lambda p: a_ref[:, sl[p]])
    kk = each(lambda p: kraw[p] * kks_ref[:, sl[p]])
    nrm = each(lambda p: group_sum(kk[p] * kk[p]))
    kk = each(lambda p: kk[p] / jnp.maximum(jnp.sqrt(nrm[p]), 1e-12))
    k = each(lambda p: kraw[p] * (1.0 + (a[p] - 1.0) * ka_ref[:, sl[p]]))
    ldp = each(lambda p: split(ld[p]))
    cum = each(lambda p: _dot(lower, ldp[p][0]) + _dot(lower, ldp[p][1]))
    tot = each(lambda p: cum[p][C - 1:C])
    beta = each(lambda p: kk[p] * a[p])
    a_s = each(lambda p: stack(-kk[p] * jnp.exp(cum[p] - ld[p])))
    r_s = each(lambda p: stack(r[p] * jnp.exp(cum[p])))
    e_inv = each(lambda p: jnp.exp(-cum[p]))
    e_rem = each(lambda p: jnp.exp(tot[p] - cum[p]))
    bt_s = each(lambda p: stack(beta[p] * e_inv[p]))
    kt_s = each(lambda p: stack(k[p] * e_inv[p]))
    bh_s = each(lambda p: stack(beta[p] * e_rem[p]))
    kh_s = each(lambda p: stack(k[p] * e_rem[p]))
    v_s = each(lambda p: stack(v[p]))
    quad = each(lambda p: _dot_nt(jnp.concatenate([a_s[p], r_s[p]], axis=0),
                                  jnp.concatenate([bt_s[p], kt_s[p]], axis=0)))
    lp = each(lambda p: jnp.where(strict, quad[p][:P2, :P2], 0.0))
    l_ak = each(lambda p: jnp.where(strict, quad[p][:P2, P2:], 0.0).astype(BF16))
    m_rb = each(lambda p: jnp.where(incl, quad[p][P2:, :P2], 0.0).astype(BF16))
    m_rk = each(lambda p: jnp.where(incl, quad[p][P2:, P2:], 0.0).astype(BF16))
    x = each(lambda p: jnp.concatenate([a_s[p].astype(F32), _dot(l_ak[p], v_s[p])], axis=1))
    steps = int(np.log2(C))
    for s in range(steps):
        lpb = each(lambda p: lp[p].astype(BF16))
        x = each(lambda p: x[p] + _dot(lpb[p], x[p].astype(BF16)))
        if s + 1 < steps:
            lp = each(lambda p: _dot(lpb[p], lpb[p]))
    xb = each(lambda p: x[p].astype(BF16))
    ht = each(lambda p: ht_sc[p])
    htb = each(lambda p: ht[p].astype(BF16))
    qo = each(lambda p: _dot(m_rb[p], xb[p]))
    q_eff = each(lambda p: (r_s[p].astype(F32) + qo[p][:, :LANES]).astype(BF16))
    o = each(lambda p: _dot_nt(q_eff[p], htb[p]) + qo[p][:, LANES:] + _dot(m_rk[p], v_s[p]))
    pgt = each(lambda p: _dot_tn(xb[p], bh_s[p]))
    for p in P:
        ht_sc[p] = (ht[p] * jnp.exp(tot[p]) + _dot(htb[p], pgt[p][:LANES].astype(BF16)) + pgt[p][LANES:]
                    + _dot_tn(v_s[p], kh_s[p]))
    o = each(lambda p: o[p][:C] + o[p][C:])
    mean = each(lambda p: group_sum(o[p]) * (1.0 / N))
    d = each(lambda p: o[p] - mean[p])
    var = each(lambda p: group_sum(d[p] * d[p]) * (1.0 / N))
    bonus = each(lambda p: group_sum(r[p] * k[p] * rk_ref[:, sl[p]]))
    for p in P:
        y = d[p] * lax.rsqrt(var[p] + RWKV_LN_EPS) * lnw_ref[:, sl[p]] + lnb_ref[:, sl[p]]
        o_ref[:, sl[p]] = ((y + bonus[p] * v[p]) * g_ref[:, sl[p]]).astype(o_ref.dtype)


def rwkv_mix(r, k, v, ld, a, g, kk_scale, ka, rk, ln_w, ln_b, B, S):
    T, W = r.shape
    C = RWKV_CHUNK
    assert 2 * RWKV_HEAD_DIM == LANES and 2 * C == LANES and S % C == 0
    pairs = 8
    bw = pairs * LANES
    nc = S // C
    seq = pl.BlockSpec((C, bw), lambda b, p, c: (b * nc + c, p))
    par = pl.BlockSpec((1, bw), lambda b, p, c: (0, p))
    return pl.pallas_call(
        functools.partial(_rwkv_kernel, pairs=pairs),
        grid=(B, W // bw, nc),
        in_specs=[seq] * 6 + [par] * 5,
        out_specs=seq,
        out_shape=jax.ShapeDtypeStruct((T, W), BF16),
        scratch_shapes=[pltpu.VMEM((pairs, LANES, LANES), F32)],
        compiler_params=_params("parallel", "parallel", "arbitrary"),
        name="rwkv_mix",
    )(r, k, v, ld, a, g, *(z.reshape(1, W) for z in (kk_scale, ka, rk, ln_w, ln_b)))


def _merge_kernel(ya_ref, yb_ref, yc_ref, pa_ref, pb_ref, pc_ref, ga_ref, gb_ref, gc_ref, o_ref):
    m = ga_ref[...].astype(F32) * _dot(ya_ref[...], pa_ref[...])
    m = m + gb_ref[...].astype(F32) * _dot(yb_ref[...], pb_ref[...])
    m = m + gc_ref[...].astype(F32) * _dot(yc_ref[...], pc_ref[...])
    o_ref[...] = m.astype(o_ref.dtype)


def branch_merge(ya, yb, yc, pa, pb, pc, gm, D):
    T = ya.shape[0]
    tm, tn = _pick(T, 1024), _pick(D, 512)
    nj = D // tn
    ysp = lambda y: pl.BlockSpec((tm, y.shape[1]), lambda i, j: (i, 0))
    psp = lambda p: pl.BlockSpec((p.shape[0], tn), lambda i, j: (0, j))
    gsp = lambda n: pl.BlockSpec((tm, tn), lambda i, j: (i, n * nj + j))
    return pl.pallas_call(
        _merge_kernel,
        grid=(T // tm, nj),
        in_specs=[ysp(ya), ysp(yb), ysp(yc), psp(pa), psp(pb), psp(pc), gsp(0), gsp(1), gsp(2)],
        out_specs=pl.BlockSpec((tm, tn), lambda i, j: (i, j)),
        out_shape=jax.ShapeDtypeStruct((T, D), BF16),
        compiler_params=_params("parallel", "arbitrary"),
        name="branch_merge",
    )(ya, yb, yc, pa, pb, pc, gm, gm, gm)


MOE_DMA_UNROLL = 32


def _moe_kernel(blk_e_ref, asg_ref, h_hbm, wg_ref, bg_ref, wu_ref, bu_ref, wd_ref, bd_ref, y_hbm,
                xbuf, ybuf, gsem, ssem, *, R, n_blk, A):
    i = pl.program_id(0)
    slot = i % 2
    D = xbuf.shape[-1]

    def per_row(fn, unroll=MOE_DMA_UNROLL):
        def body(r, c):
            fn(r)
            return c
        lax.fori_loop(0, R, body, 0, unroll=unroll)

    def start_gather(step, s, unroll=MOE_DMA_UNROLL):
        def one(r):
            tok = jnp.minimum(asg_ref[step * R + r] >> TOP_K_SHIFT, A // TOP_K - 1)
            pltpu.make_async_copy(h_hbm.at[pl.ds(tok, 1)], xbuf.at[s, pl.ds(r, 1)], gsem.at[s]).start()
        per_row(one, unroll)

    def start_scatter(step, s):
        def one(r):
            dst = asg_ref[step * R + r]
            col = pl.multiple_of((dst & (TOP_K - 1)) * D, LANES)
            pltpu.make_async_copy(ybuf.at[s, pl.ds(r, 1)],
                                  y_hbm.at[pl.ds(dst >> TOP_K_SHIFT, 1), pl.ds(col, D)], ssem.at[s]).start()
        per_row(one)

    def wait_gather(s):
        pltpu.make_async_copy(h_hbm.at[pl.ds(0, R)], xbuf.at[s], gsem.at[s]).wait()

    def wait_scatter(s):
        pltpu.make_async_copy(ybuf.at[s], y_hbm.at[pl.ds(0, R), pl.ds(0, D)], ssem.at[s]).wait()

    @pl.when(i == 0)
    def _():
        start_gather(0, 0)

    wait_gather(slot)
    x = xbuf[slot].astype(BF16)
    start_gather(jnp.minimum(i + 1, n_blk - 1), 1 - slot, unroll=True)
    gt = jnp.minimum(_dot(x, wg_ref[...]) + bg_ref[...], SWIGLU_LIMIT)
    up = jnp.clip(_dot(x, wu_ref[...]) + bu_ref[...], -SWIGLU_LIMIT, SWIGLU_LIMIT)
    act = gt * jax.nn.sigmoid(SWIGLU_ALPHA * gt) * (up + 1.0)
    y = _dot(act.astype(BF16), wd_ref[...]) + bd_ref[...]

    @pl.when(i >= 2)
    def _():
        wait_scatter(slot)

    ybuf[slot] = y
    start_scatter(i, slot)

    @pl.when(i == n_blk - 1)
    def _():
        wait_gather(1 - slot)
        wait_scatter(1 - slot)
        wait_scatter(slot)


def moe_experts(h, blk_e, row_asg, wg, bg, wu, bu, wd, bd, R):
    T, D = h.shape
    E, _, F = wg.shape
    n_blk = blk_e.shape[0]
    A = T * TOP_K
    assert n_blk >= 2 and (n_blk * R) % TOP_K == 0 and 1 << TOP_K_SHIFT == TOP_K
    wspec = lambda shape: pl.BlockSpec((None,) + shape, lambda i, be, ra: (be[i], 0, 0))
    return pl.pallas_call(
        functools.partial(_moe_kernel, R=R, n_blk=n_blk, A=A),
        grid_spec=pltpu.PrefetchScalarGridSpec(
            num_scalar_prefetch=2,
            grid=(n_blk,),
            in_specs=[pl.BlockSpec(memory_space=pl.ANY),
                      wspec((D, F)), wspec((1, F)), wspec((D, F)), wspec((1, F)),
                      wspec((F, D)), wspec((1, D))],
            out_specs=pl.BlockSpec(memory_space=pl.ANY),
            scratch_shapes=[pltpu.VMEM((2, R, D), F32), pltpu.VMEM((2, R, D), F32),
                            pltpu.SemaphoreType.DMA((2,)), pltpu.SemaphoreType.DMA((2,))]),
        out_shape=jax.ShapeDtypeStruct((n_blk * R // TOP_K, TOP_K * D), F32),
        compiler_params=_params("arbitrary"),
        name="moe_experts",
    )(blk_e, row_asg, h, wg, bg.reshape(E, 1, F), wu, bu.reshape(E, 1, F), wd, bd.reshape(E, 1, D))


def _combined(x_ref, y_ref, wt_ref, gate_ref):
    D = x_ref.shape[1]
    wt = wt_ref[...]
    acc = wt[:, 0:1] * y_ref[:, 0:D]
    for k in range(1, TOP_K):
        acc = acc + wt[:, k:k + 1] * y_ref[:, k * D:(k + 1) * D]
    return x_ref[...] + gate_ref[0] * acc


def _combine_next_kernel(x_ref, y_ref, wt_ref, gate_ref, g_ref, sc_ref, sh_ref, o_ref, h_ref):
    x = _combined(x_ref, y_ref, wt_ref, gate_ref)
    o_ref[...] = x
    h_ref[...] = _norm_mod(x, g_ref[...], sc_ref[0], sh_ref[0]).astype(h_ref.dtype)


def _combine_final_kernel(x_ref, y_ref, wt_ref, gate_ref, g_ref, o_ref):
    x = _combined(x_ref, y_ref, wt_ref, gate_ref)
    ms = jnp.mean(x * x, axis=-1, keepdims=True)
    o_ref[...] = x * lax.rsqrt(ms + RMS_EPS) * g_ref[...]


def moe_combine(x, y, wt, gate, S, norm_g, scale=None, shift=None):
    T, D = x.shape
    B = gate.shape[0]
    tm = _pick(S, 128)
    spt = S // tm
    row = pl.BlockSpec((tm, D), lambda i: (i, 0))
    per_batch = pl.BlockSpec((1, 1, D), lambda i: (i // spt, 0, 0))
    in_specs = [row, pl.BlockSpec((tm, TOP_K * D), lambda i: (i, 0)),
                pl.BlockSpec((tm, LANES), lambda i: (i, 0)), per_batch,
                pl.BlockSpec((1, D), lambda i: (0, 0))]
    args = [x, y, wt, gate.reshape(B, 1, D), norm_g.reshape(1, D)]
    if scale is None:
        return pl.pallas_call(
            _combine_final_kernel, grid=(T // tm,), in_specs=in_specs, out_specs=row,
            out_shape=jax.ShapeDtypeStruct((T, D), F32),
            compiler_params=_params("parallel"), name="moe_combine_final")(*args)
    return pl.pallas_call(
        _combine_next_kernel, grid=(T // tm,), in_specs=in_specs + [per_batch, per_batch],
        out_specs=[row, row],
        out_shape=[jax.ShapeDtypeStruct((T, D), F32), jax.ShapeDtypeStruct((T, D), BF16)],
        compiler_params=_params("parallel"), name="moe_combine_next",
    )(*args, scale.reshape(B, 1, D), shift.reshape(B, 1, D))


def route_tables(top_idx, rank, counts, E, R):
    T = top_idx.shape[0]
    A = T * TOP_K
    flat_e = top_idx.reshape(A)
    padded = (counts + R - 1) // R * R
    pends = jnp.cumsum(padded)
    onehot = flat_e[:, None] == jnp.arange(E, dtype=I32)[None, :]
    dest = jnp.sum(jnp.where(onehot, (pends - padded)[None, :], 0), axis=1) + rank.reshape(A)
    n_blk = -(-(A + E * (R - 1)) // R)
    blk_start = jnp.arange(n_blk, dtype=I32)[:, None] * R
    blk_e = jnp.minimum(jnp.sum((pends[None, :] <= blk_start).astype(I32), axis=1), E - 1)
    pads_before = jnp.cumsum(padded - counts) - (padded - counts)
    r = blk_start + jnp.arange(R, dtype=I32)[None, :] - (pends - padded)[blk_e][:, None]
    over = r - counts[blk_e][:, None]
    pad_id = (A + pads_before[blk_e][:, None] + over).reshape(n_blk * R)
    real = jnp.full((n_blk * R,), -1, I32).at[dest].set(jnp.arange(A, dtype=I32))
    row_asg = jnp.where(real < 0, pad_id, real)
    return blk_e, row_asg


def _mixer_weights(w_in, D):
    CW, QW, GW = CONV_WIDTH, NSA_Q_WIDTH, NSA_G_WIDTH
    o_q = 3 * CW
    o_kv = o_q + QW
    o_g = o_kv + 6 * GW
    o_r = o_g + NSA_GATE_WIDTH
    o_m = o_r + RWKV_IN_WIDTH
    w_qkv = w_in[:, :o_g].astype(BF16)
    per_group = NSA_GATE_WIDTH // NSA_KV_GROUPS
    gate_blocks = [jnp.pad(w_in[:, o_g + g * per_group:o_g + (g + 1) * per_group],
                           ((0, 0), (0, LANES - per_group))) for g in range(NSA_KV_GROUPS)]
    w_rg = jnp.concatenate(
        [jnp.pad(w_in[:, o_r:o_m], ((0, 0), (0, RWKV_PAD_WIDTH - RWKV_IN_WIDTH)))] + gate_blocks,
        axis=1).astype(BF16)
    w_m = w_in[:, o_m:].astype(BF16)
    return w_qkv, w_rg, w_m


def _layer(x, h, mod, next_norm, final_g, B, S, tables, norm_g, w_in, conv_w, cmp_pe, cmp_w1, cmp_w2, rwkv_mu, rwkv_w0,
           rwkv_wb, rwkv_a0, rwkv_ab, rwkv_gb, rwkv_kk, rwkv_ka, rwkv_rk, rwkv_ln_w, rwkv_ln_b,
           w_branch, w_out, router_w, router_b, exp_wg, exp_bg, exp_wu, exp_bu, exp_wd, exp_bd):
    T, D = x.shape
    CW, QW, GW, Dh, G = CONV_WIDTH, NSA_Q_WIDTH, NSA_G_WIDTH, NSA_HEAD_DIM, NSA_KV_GROUPS
    shift1, scale1, gate1, shift2, scale2, gate2 = (mod[:, n] for n in range(N_MOD))

    w_qkv, w_rg, w_m = _mixer_weights(w_in, D)
    z = mm_qkv(h, w_qkv, tables, S)
    zrg = _mm_call(_mm_plain_kernel, h, w_rg, F32, name="mm_rwkv_gate")
    gm = _mm_call(_mm_sigmoid_kernel, h, w_m, BF16, name="mm_merge_gate")

    y_a = gated_conv(z, conv_w, S)

    ns = S // CMP_STRIDE
    kv0 = 3 * CW + QW
    to_strides = lambda zz: zz.reshape(B, ns, CMP_STRIDE, G, Dh).transpose(0, 3, 1, 2, 4).reshape(
        B, G, ns, CMP_STRIDE * Dh)
    cmp_in = jnp.stack([to_strides(z[:, kv0:kv0 + GW]), to_strides(z[:, kv0 + GW:kv0 + 2 * GW])])
    kv_cmp = compress_kv(cmp_in, cmp_pe, cmp_w1, cmp_w2)
    y_b = nsa_attention(z, zrg, kv_cmp, B, S, q_col=3 * CW, ks_col=kv0 + 2 * GW, vs_col=kv0 + 3 * GW,
                        kw_col=kv0 + 4 * GW, vw_col=kv0 + 5 * GW, gate_col=RWKV_PAD_WIDTH)

    r, k, v, ld, a, g = rwkv_prep(zrg, rwkv_mu, rwkv_w0, rwkv_wb, rwkv_a0, rwkv_ab, rwkv_gb, S)
    y_c = rwkv_mix(r, k, v, ld, a, g, rwkv_kk, rwkv_ka, rwkv_rk.reshape(-1), rwkv_ln_w, rwkv_ln_b, B, S)

    wb16 = w_branch.astype(BF16)
    m = branch_merge(y_a, y_b, y_c, wb16[:CW], wb16[CW:CW + QW], wb16[CW + QW:], gm, D)
    x = mm_residual(m, w_out.astype(BF16), x, gate1, S)

    E = router_w.shape[1]
    h2, top_idx, top_w, counts = norm_router(x, norm_g[1], scale2, shift2, router_w, router_b, S)
    R = _pick(T * TOP_K // E, 256)
    blk_e, row_asg = route_tables(top_idx[:, :TOP_K], top_idx[:, TOP_K:2 * TOP_K], counts[0, :E], E, R)
    y = moe_experts(h2, blk_e, row_asg, exp_wg.astype(BF16), exp_bg, exp_wu.astype(BF16), exp_bu,
                    exp_wd.astype(BF16), exp_bd, R)
    if next_norm is None:
        return moe_combine(x, y, top_w, gate2, S, final_g)
    return moe_combine(x, y, top_w, gate2, S, *next_norm)


def kernel(x, c, ada_w, ada_b, ada_table, norm_g, final_g, w_in, conv_w, cmp_pe, cmp_w1, cmp_w2, rwkv_mu, rwkv_w0, rwkv_wb, rwkv_a0, rwkv_ab, rwkv_gb, rwkv_kk, rwkv_ka, rwkv_rk, rwkv_ln_w, rwkv_ln_b, w_branch, w_out, router_w, router_b, exp_wg, exp_bg, exp_wu, exp_bu, exp_wd, exp_bd):
    B, S, D = x.shape
    depth = w_in.shape[0]
    tables = rope_lane_tables(S)
    mod_all = ada_modulation(c, ada_w, ada_b).reshape(B, N_MOD, D)
    xf = x.reshape(B * S, D)
    mods = [mod_all + ada_table[l] for l in range(depth)]
    h = norm_modulate(xf, norm_g[0, 0], mods[0][:, 1], mods[0][:, 0], S, BF16)
    for l in range(depth):
        next_norm = None if l + 1 == depth else (norm_g[l + 1, 0], mods[l + 1][:, 1], mods[l + 1][:, 0])
        out = _layer(xf, h, mods[l], next_norm, final_g, B, S, tables, norm_g[l], w_in[l], conv_w[l],
                     cmp_pe[l], cmp_w1[l], cmp_w2[l], rwkv_mu[l], rwkv_w0[l], rwkv_wb[l], rwkv_a0[l],
                     rwkv_ab[l], rwkv_gb[l], rwkv_kk[l], rwkv_ka[l], rwkv_rk[l], rwkv_ln_w[l],
                     rwkv_ln_b[l], w_branch[l], w_out[l], router_w[l], router_b[l], exp_wg[l], exp_bg[l],
                     exp_wu[l], exp_bu[l], exp_wd[l], exp_bd[l])
        if next_norm is None:
            return out.reshape(B, S, D)
        xf, h = out
```

```python
import functools

import numpy as np
import jax
import jax.numpy as jnp
from jax import lax
from jax.experimental import pallas as pl
from jax.experimental.pallas import tpu as pltpu

F32 = jnp.float32
BF16 = jnp.bfloat16
I32 = jnp.int32
HI = lax.Precision.HIGHEST

N_MOD = 6
RMS_EPS = 1e-6
CONV_WIDTH = 1024
CONV_K = 3
NSA_HEADS = 16
NSA_KV_GROUPS = 4
NSA_HPG = NSA_HEADS // NSA_KV_GROUPS
NSA_HEAD_DIM = 128
CMP_BLOCK = 32
CMP_STRIDE = 16
SEL_BLOCK = 64
N_SELECT = 16
WINDOW = 512
FORCE_SCORE = 1e9
NEG_INF = -1e30
ROPE_THETA = 500000.0
ROPE_DIMS = NSA_HEAD_DIM // 4
NSA_Q_WIDTH = NSA_HEADS * NSA_HEAD_DIM
NSA_G_WIDTH = NSA_KV_GROUPS * NSA_HEAD_DIM
NSA_GATE_WIDTH = 3 * NSA_HEADS
RWKV_HEADS = 16
RWKV_HEAD_DIM = 64
RWKV_WIDTH = RWKV_HEADS * RWKV_HEAD_DIM
RWKV_DECAY_LORA = 64
RWKV_A_LORA = 64
RWKV_GATE_LORA = 160
RWKV_IN_WIDTH = 3 * RWKV_WIDTH + RWKV_DECAY_LORA + RWKV_A_LORA + RWKV_GATE_LORA
RWKV_LN_EPS = 64e-5
RWKV_CHUNK = 64
TOP_K = 4
TOP_K_SHIFT = 2
SWIGLU_ALPHA = 1.702
SWIGLU_LIMIT = 7.0

LANES = 128
SUBLANES = 8
VMEM_BYTES_V7X = 64 * 1024 * 1024
VMEM_LIMIT = VMEM_BYTES_V7X - 12 * 1024 * 1024

RWKV_PAD_WIDTH = -(-RWKV_IN_WIDTH // 512) * 512
RWKV_TAIL = 3 * RWKV_WIDTH


def _params(*sem):
    return pltpu.CompilerParams(dimension_semantics=sem, vmem_limit_bytes=VMEM_LIMIT)


def _pick(n, pref):
    t = min(n, pref)
    while n % t:
        t //= 2
    return t


def _dot(a, b, **kw):
    return jnp.dot(a, b, preferred_element_type=F32, **kw)


def _dot_nt(a, b, **kw):
    return lax.dot_general(a, b, (((1,), (1,)), ((), ())), preferred_element_type=F32, **kw)


def _dot_tn(a, b, **kw):
    return lax.dot_general(a, b, (((0,), (0,)), ((), ())), preferred_element_type=F32, **kw)


def _ada_kernel(c_ref, w_ref, b_ref, o_ref):
    c = c_ref[...]
    sc = c * jax.nn.sigmoid(c)
    o_ref[...] = _dot(sc, w_ref[...], precision=HI) + b_ref[...]


def ada_modulation(c, ada_w, ada_b):
    B, D = c.shape
    N = ada_w.shape[1]
    rows = -(-B // SUBLANES) * SUBLANES
    cp = jnp.zeros((rows, D), F32).at[:B].set(c)
    tn = _pick(N, 512)
    out = pl.pallas_call(
        _ada_kernel,
        grid=(N // tn,),
        in_specs=[pl.BlockSpec((rows, D), lambda j: (0, 0)),
                  pl.BlockSpec((D, tn), lambda j: (0, j)),
                  pl.BlockSpec((1, tn), lambda j: (0, j))],
        out_specs=pl.BlockSpec((rows, tn), lambda j: (0, j)),
        out_shape=jax.ShapeDtypeStruct((rows, N), F32),
        compiler_params=_params("parallel"),
        name="ada_mod",
    )(cp, ada_w, ada_b.reshape(1, N))
    return out[:B]


def _norm_mod(x, g, sc, sh):
    ms = jnp.mean(x * x, axis=-1, keepdims=True)
    y = x * lax.rsqrt(ms + RMS_EPS) * g
    return y * (1.0 + sc) + sh


def _norm_mod_kernel(x_ref, g_ref, sc_ref, sh_ref, o_ref):
    h = _norm_mod(x_ref[...], g_ref[...], sc_ref[0], sh_ref[0])
    o_ref[...] = h.astype(o_ref.dtype)


def norm_modulate(x, g, scale, shift, S, out_dtype):
    T, D = x.shape
    B = scale.shape[0]
    tm = _pick(S, 256)
    spt = S // tm
    return pl.pallas_call(
        _norm_mod_kernel,
        grid=(T // tm,),
        in_specs=[pl.BlockSpec((tm, D), lambda i: (i, 0)),
                  pl.BlockSpec((1, D), lambda i: (0, 0)),
                  pl.BlockSpec((1, 1, D), lambda i: (i // spt, 0, 0)),
                  pl.BlockSpec((1, 1, D), lambda i: (i // spt, 0, 0))],
        out_specs=pl.BlockSpec((tm, D), lambda i: (i, 0)),
        out_shape=jax.ShapeDtypeStruct((T, D), out_dtype),
        compiler_params=_params("parallel"),
        name="norm_mod",
    )(x, g.reshape(1, D), scale.reshape(B, 1, D), shift.reshape(B, 1, D))


def _norm_router_kernel(x_ref, g_ref, sc_ref, sh_ref, rw_ref, rb_ref, h_ref, idx_ref, wt_ref, cnt_ref,
                        base_sc):
    @pl.when(pl.program_id(0) == 0)
    def _():
        base_sc[...] = jnp.zeros(base_sc.shape, F32)

    h = _norm_mod(x_ref[...], g_ref[...], sc_ref[0], sh_ref[0])
    nj = h.shape[1] // LANES
    for j in range(nj):
        h_ref[pl.ds(j, h.shape[0], stride=nj), :] = h[:, j * LANES:(j + 1) * LANES]
    logits = _dot(h, rw_ref[...], precision=HI) + rb_ref[...]
    tm = logits.shape[0]
    lane = lax.broadcasted_iota(I32, logits.shape, 1)
    lanef = lane.astype(F32)
    idx_out = jnp.zeros(logits.shape, I32)
    val_out = jnp.zeros(logits.shape, F32)
    v0 = None
    hits = []
    for k in range(TOP_K):
        mx = jnp.max(logits, axis=1, keepdims=True)
        first = jnp.min(jnp.where(logits == mx, lanef, float(LANES)), axis=1, keepdims=True)
        hit = lanef == first
        hits.append(jnp.where(hit, 1.0, 0.0))
        if k == 0:
            v0 = mx
        idx_out = jnp.where(lane == k, first.astype(I32), idx_out)
        val_out = jnp.where(lane == k, jnp.exp(mx - v0), val_out)
        logits = jnp.where(hit, -3e38, logits)
    wt_ref[...] = val_out / jnp.sum(val_out, axis=1, keepdims=True)
    used = hits[0]
    for k in range(1, TOP_K):
        used = used + hits[k]
    earlier = jnp.where(lax.broadcasted_iota(I32, (tm, tm), 1) < lax.broadcasted_iota(I32, (tm, tm), 0),
                        1.0, 0.0).astype(BF16)
    before = base_sc[0:1] + _dot(earlier, used.astype(BF16))
    for k in range(TOP_K):
        rank = jnp.sum(hits[k] * before, axis=1, keepdims=True)
        idx_out = jnp.where(lane == TOP_K + k, rank.astype(I32), idx_out)
    idx_ref[...] = idx_out
    total = base_sc[0:1] + jnp.sum(used, axis=0, keepdims=True)
    base_sc[...] = jnp.broadcast_to(total, base_sc.shape)
    cnt_ref[...] = jnp.broadcast_to(total, cnt_ref.shape).astype(I32)


def norm_router(x, g, scale, shift, router_w, router_b, S):
    T, D = x.shape
    B = scale.shape[0]
    E = router_w.shape[1]
    assert E <= LANES
    rw = jnp.zeros((D, LANES), F32).at[:, :E].set(router_w)
    rb = jnp.full((1, LANES), NEG_INF, F32).at[0, :E].set(router_b)
    tm = _pick(S, 256)
    spt = S // tm
    return pl.pallas_call(
        _norm_router_kernel,
        grid=(T // tm,),
        in_specs=[pl.BlockSpec((tm, D), lambda i: (i, 0)),
                  pl.BlockSpec((1, D), lambda i: (0, 0)),
                  pl.BlockSpec((1, 1, D), lambda i: (i // spt, 0, 0)),
                  pl.BlockSpec((1, 1, D), lambda i: (i // spt, 0, 0)),
                  pl.BlockSpec((D, LANES), lambda i: (0, 0)),
                  pl.BlockSpec((1, LANES), lambda i: (0, 0))],
        out_specs=[pl.BlockSpec((tm * (D // LANES), LANES), lambda i: (i, 0)),
                   pl.BlockSpec((tm, LANES), lambda i: (i, 0)),
                   pl.BlockSpec((tm, LANES), lambda i: (i, 0)),
                   pl.BlockSpec((SUBLANES, LANES), lambda i: (0, 0))],
        out_shape=[jax.ShapeDtypeStruct((T * (D // LANES), LANES), F32),
                   jax.ShapeDtypeStruct((T, LANES), I32),
                   jax.ShapeDtypeStruct((T, LANES), F32),
                   jax.ShapeDtypeStruct((SUBLANES, LANES), I32)],
        scratch_shapes=[pltpu.VMEM((SUBLANES, LANES), F32)],
        compiler_params=_params("arbitrary"),
        name="norm_router",
    )(x, g.reshape(1, D), scale.reshape(B, 1, D), shift.reshape(B, 1, D), rw, rb)


def _rope_lanes(x, cf, sa, sb):
    half = ROPE_DIMS // 2
    return x * cf + pltpu.roll(x, LANES - half, 1) * sa + pltpu.roll(x, half, 1) * sb


def _mm_plain_kernel(a_ref, b_ref, o_ref):
    o_ref[...] = _dot(a_ref[...], b_ref[...]).astype(o_ref.dtype)


def _mm_sigmoid_kernel(a_ref, b_ref, o_ref):
    o_ref[...] = jax.nn.sigmoid(_dot(a_ref[...], b_ref[...])).astype(o_ref.dtype)


def _mm_qkv_kernel(a_ref, b_ref, cf_ref, sa_ref, sb_ref, o_ref, *, q_lo, q_hi, kv_lo):
    j = pl.program_id(1)
    is_q = (j >= q_lo) & (j < q_hi)
    is_key = (j >= kv_lo) & ((j - kv_lo) % 2 == 0)
    acc = _dot(a_ref[...], b_ref[...])

    @pl.when(is_q | is_key)
    def _():
        fac = jnp.where(is_q, NSA_HEAD_DIM ** -0.5, 1.0)
        cf, sa, sb = cf_ref[...], sa_ref[...], sb_ref[...]
        for n in range(acc.shape[1] // LANES):
            sl = slice(n * LANES, (n + 1) * LANES)
            o_ref[:, sl] = _rope_lanes(acc[:, sl] * fac, cf, sa, sb).astype(o_ref.dtype)

    @pl.when(jnp.logical_not(is_q | is_key))
    def _():
        o_ref[...] = acc.astype(o_ref.dtype)


def _mm_residual_kernel(a_ref, b_ref, x_ref, gate_ref, o_ref):
    o_ref[...] = x_ref[...] + gate_ref[0] * _dot(a_ref[...], b_ref[...])


def _mm_call(kernel, a, b, out_dtype, extra=(), extra_specs=(), tm_pref=1024, tn_pref=512, name="mm",
             layer=None, n_cols=None):
    M, K = a.shape
    N = b.shape[1] if layer is None else n_cols
    tm, tn = _pick(M, tm_pref), _pick(N, tn_pref)
    if layer is None:
        b_spec = pl.BlockSpec((K, tn), lambda i, j: (0, j))
    else:
        b_spec = pl.BlockSpec((None, K, tn), lambda i, j: (layer, 0, j))
    return pl.pallas_call(
        kernel,
        grid=(M // tm, N // tn),
        in_specs=[pl.BlockSpec((tm, K), lambda i, j: (i, 0)), b_spec] + [s(tm, tn) for s in extra_specs],
        out_specs=pl.BlockSpec((tm, tn), lambda i, j: (i, j)),
        out_shape=jax.ShapeDtypeStruct((M, N), out_dtype),
        compiler_params=_params("parallel", "arbitrary"),
        name=name,
    )(a, b, *extra)


def rope_lane_tables(S):
    half = ROPE_DIMS // 2
    inv = ROPE_THETA ** (-jnp.arange(0, ROPE_DIMS, 2, dtype=F32) / ROPE_DIMS)
    ang = jnp.arange(S, dtype=F32)[:, None] * inv[None, :]
    cos, sin = jnp.cos(ang), jnp.sin(ang)
    zeros = jnp.zeros((S, LANES - ROPE_DIMS), F32)
    zh = jnp.zeros((S, half), F32)
    cf = jnp.concatenate([cos, cos, zeros + 1.0], axis=1)
    sa = jnp.concatenate([-sin, zh, zeros], axis=1)
    sb = jnp.concatenate([zh, sin, zeros], axis=1)
    return cf, sa, sb


def mm_qkv(a, w_stack, layer, tables, S):
    T = a.shape[0]
    tm = _pick(min(T, S), 1024)
    spt = S // tm
    spec = lambda tm_, tn_: pl.BlockSpec((tm_, LANES), lambda i, j: (i % spt, 0))
    tn = NSA_G_WIDTH
    q0, kv0 = 3 * CONV_WIDTH, 3 * CONV_WIDTH + NSA_Q_WIDTH
    assert q0 % tn == 0 and kv0 % tn == 0
    kern = functools.partial(_mm_qkv_kernel, q_lo=q0 // tn, q_hi=kv0 // tn, kv_lo=kv0 // tn)
    return _mm_call(kern, a, w_stack, BF16, extra=tables, extra_specs=(spec,) * 3, tm_pref=tm, tn_pref=tn,
                    name="mm_qkv", layer=layer, n_cols=kv0 + 6 * tn)


def mm_residual(a, b, x, gate, S):
    T = a.shape[0]
    B, D = gate.shape
    tm = _pick(min(T, S), 1024)
    spt = S // tm
    xspec = lambda tm_, tn_: pl.BlockSpec((tm_, tn_), lambda i, j: (i, j))
    gspec = lambda tm_, tn_: pl.BlockSpec((1, 1, tn_), lambda i, j: (i // spt, 0, j))
    return _mm_call(_mm_residual_kernel, a, b, F32, extra=(x, gate.reshape(B, 1, D)),
                    extra_specs=(xspec, gspec), tm_pref=tm, name="mm_residual")


def _conv_kernel(z_ref, zp_ref, w_ref, o_ref, *, tm, S):
    i = pl.program_id(0)
    W = CONV_WIDTH
    z = z_ref[...].astype(F32)
    cb, u = z[:, :W], z[:, W:2 * W] * z[:, 2 * W:]
    zp = zp_ref[...].astype(F32)
    up = zp[:, W:2 * W] * zp[:, 2 * W:]
    up = jnp.where((i * tm) % S == 0, 0.0, up)
    row = lax.broadcasted_iota(I32, u.shape, 0)
    last, prev = up[SUBLANES - 1:SUBLANES], up[SUBLANES - 2:SUBLANES - 1]
    u1 = jnp.where(row == 0, last, pltpu.roll(u, 1, 0))
    u2 = jnp.where(row == 0, prev, jnp.where(row == 1, last, pltpu.roll(u, 2, 0)))
    w = w_ref[...]
    o_ref[...] = (cb * (w[0:1] * u2 + w[1:2] * u1 + w[2:3] * u)).astype(o_ref.dtype)


def gated_conv(z, conv_w, S):
    T = z.shape[0]
    W = CONV_WIDTH
    assert CONV_K == 3
    tm = _pick(S, 512)
    wp = jnp.zeros((SUBLANES, W), F32).at[:CONV_K].set(conv_w)
    return pl.pallas_call(
        functools.partial(_conv_kernel, tm=tm, S=S),
        grid=(T // tm,),
        in_specs=[pl.BlockSpec((tm, 3 * W), lambda i: (i, 0)),
                  pl.BlockSpec((SUBLANES, 3 * W), lambda i: (jnp.maximum(i * (tm // SUBLANES) - 1, 0), 0)),
                  pl.BlockSpec((SUBLANES, W), lambda i: (0, 0))],
        out_specs=pl.BlockSpec((tm, W), lambda i: (i, 0)),
        out_shape=jax.ShapeDtypeStruct((T, W), BF16),
        compiler_params=_params("parallel"),
        name="gated_conv",
    )(z, z, wp)


def _compress_kernel(x_ref, w1_ref, pe_ref, w2_ref, o_ref):
    x = x_ref[...]
    w1 = w1_ref[...]
    half = x.shape[1]
    ns = x.shape[0]
    first = _dot(x, w1[:half].astype(BF16))
    second = _dot(x, w1[half:].astype(BF16))
    bias = _dot(pe_ref[...], w1, precision=HI)[0:1]
    hid = first + pltpu.roll(second, ns - 1, 0) + bias
    act = jax.nn.gelu(hid, approximate=True)
    o_ref[...] = _dot(act.astype(BF16), w2_ref[...].astype(BF16)).astype(o_ref.dtype)


def compress_kv(x, cmp_pe, cmp_w1, cmp_w2):
    _, B, G, NS, W = x.shape
    assert CMP_BLOCK == 2 * CMP_STRIDE
    Dh = NSA_HEAD_DIM
    hid = cmp_w1.shape[-1]
    pe = jnp.broadcast_to(cmp_pe.reshape(2, 1, CMP_BLOCK * Dh), (2, SUBLANES, CMP_BLOCK * Dh))
    return pl.pallas_call(
        _compress_kernel,
        grid=(2, B, G),
        in_specs=[pl.BlockSpec((None, None, None, NS, W), lambda s, b, g: (s, b, g, 0, 0)),
                  pl.BlockSpec((None, CMP_BLOCK * Dh, hid), lambda s, b, g: (s, 0, 0)),
                  pl.BlockSpec((None, SUBLANES, CMP_BLOCK * Dh), lambda s, b, g: (s, 0, 0)),
                  pl.BlockSpec((None, hid, Dh), lambda s, b, g: (s, 0, 0))],
        out_specs=pl.BlockSpec((None, None, None, NS, Dh), lambda s, b, g: (s, b, g, 0, 0)),
        out_shape=jax.ShapeDtypeStruct((2, B, G, NS, Dh), BF16),
        compiler_params=_params("parallel", "parallel", "parallel"),
        name="nsa_compress",
    )(x, cmp_w1, pe, cmp_w2)


def _nsa_kernel(q_ref, kc_ref, vc_ref, ks_ref, vs_ref, kw_ref, vw_ref, zg_ref, bmapt_ref, o_ref,
                s_sc, m_sc, acc_sc, out_sc, *, tq, tk, nsel):
    t0 = pl.program_id(2) * tq
    Dh = NSA_HEAD_DIM
    hpg = NSA_HPG
    nch = tk // LANES
    heads = range(hpg)
    trow = t0 + lax.broadcasted_iota(I32, (tq, 1), 0)
    gates = jax.nn.sigmoid(zg_ref[...])

    def q(h):
        return q_ref[:, h * Dh:(h + 1) * Dh]

    def gate(h, branch):
        return gates[:, 3 * h + branch:3 * h + branch + 1]

    kc, vc = kc_ref[...], vc_ref[...]
    ncb = kc.shape[0]
    blk_end = lax.broadcasted_iota(I32, (1, ncb), 1) * CMP_STRIDE + (CMP_BLOCK - 1)
    valid = blk_end <= trow
    bias_c = jnp.where(valid, 0.0, NEG_INF)
    validf = jnp.where(valid, 1.0, 0.0)
    psum = None
    for h in heads:
        sm = _dot_nt(q(h), kc) + bias_c
        p = jnp.exp(sm - jnp.max(sm, axis=1, keepdims=True)) * validf
        pc = p / jnp.maximum(jnp.sum(p, axis=1, keepdims=True), 1e-30)
        out_sc[h] = gate(h, 0) * _dot(pc.astype(BF16), vc)
        psum = pc if psum is None else psum + pc

    bmt = bmapt_ref[...]
    hi = psum.astype(BF16)
    rest = psum - hi.astype(F32)
    mid = rest.astype(BF16)
    lo = (rest - mid.astype(F32)).astype(BF16)
    imp = _dot_nt(bmt, hi) + _dot_nt(bmt, mid) + _dot_nt(bmt, lo)
    nb = imp.shape[0]
    jb = lax.broadcasted_iota(I32, (nb, tq), 0)
    cur = (t0 + lax.broadcasted_iota(I32, (1, tq), 1)) // SEL_BLOCK
    forced = (jb == 0) | (jb == cur) | (jb == cur - 1)
    imp = jnp.where(forced, FORCE_SCORE, imp)
    imp = jnp.where(jb <= cur, imp, NEG_INF)
    rank = jnp.zeros((nb, tq), F32)
    for i in range(nb):
        ri = imp[i:i + 1, :]
        rank = rank + jnp.where((ri > imp) | ((ri == imp) & (jb > i)), 1.0, 0.0)
    sel_t = jnp.where(rank < nsel, 1.0, 0.0)

    bpt = tk // SEL_BLOCK
    nkt = (t0 + tq + tk - 1) // tk
    m_sc[...] = jnp.full(m_sc.shape, NEG_INF, F32)
    acc_sc[...] = jnp.zeros(acc_sc.shape, F32)

    def chunks(a):
        return [a[:, c * LANES:(c + 1) * LANES] for c in range(nch)]

    def score_sweep(kt, carry):
        k = ks_ref[pl.ds(pl.multiple_of(kt * tk, tk), tk), :]
        expand = jnp.where(lax.broadcasted_iota(I32, (nb, tk), 0)
                           == kt * bpt + lax.broadcasted_iota(I32, (nb, tk), 1) // SEL_BLOCK, 1.0, 0.0)
        chosen = _dot_tn(sel_t, expand)
        kpos = kt * tk + lax.broadcasted_iota(I32, (1, tk), 1)
        bias = jnp.where((chosen > 0.5) & (kpos <= trow), 0.0, NEG_INF)
        for h in heads:
            s = _dot_nt(q(h), k) + bias
            s_sc[kt, h] = s
            m = m_sc[h]
            for sc in chunks(s):
                m = jnp.maximum(m, sc)
            m_sc[h] = m
        return carry

    lax.fori_loop(0, nkt, score_sweep, 0)
    for h in heads:
        m_sc[h] = jnp.broadcast_to(jnp.max(m_sc[h], axis=1, keepdims=True), (tq, LANES))
    ones = jnp.ones((tk, LANES), BF16)

    def exp_sweep(kt, carry):
        v1 = jnp.concatenate([vs_ref[pl.ds(pl.multiple_of(kt * tk, tk), tk), :], ones], axis=1)
        for h in heads:
            m = m_sc[h]
            p = jnp.concatenate([jnp.exp(sc - m) for sc in chunks(s_sc[kt, h])], axis=1)
            acc_sc[h] = acc_sc[h] + _dot(p.astype(BF16), v1)
        return carry

    lax.fori_loop(0, nkt, exp_sweep, 0)
    for h in heads:
        acc = acc_sc[h]
        out_sc[h] = out_sc[h] + gate(h, 1) * (acc[:, :Dh] / acc[:, Dh:Dh + 1])

    span = WINDOW + tq
    w0 = pl.multiple_of(jnp.maximum(t0 - WINDOW, 0), tq)
    kwv = kw_ref[pl.ds(w0, span), :]
    vwv = vw_ref[pl.ds(w0, span), :]
    diff = trow - (w0 + lax.broadcasted_iota(I32, (1, span), 1))
    bias_w = jnp.where((diff >= 0) & (diff < WINDOW), 0.0, NEG_INF)
    for h in heads:
        sw = _dot_nt(q(h), kwv) + bias_w
        pw = jnp.exp(sw - jnp.max(sw, axis=1, keepdims=True))
        o_w = _dot(pw.astype(BF16), vwv) / jnp.sum(pw, axis=1, keepdims=True)
        o_ref[:, h * Dh:(h + 1) * Dh] = (out_sc[h] + gate(h, 2) * o_w).astype(o_ref.dtype)


def block_map(nc, nb):
    i = np.arange(nc)[:, None] * CMP_STRIDE
    j = np.arange(nb)[None, :] * SEL_BLOCK
    inter = np.clip(np.minimum(i + CMP_BLOCK, j + SEL_BLOCK) - np.maximum(i, j), 0, None)
    return jnp.asarray(inter.astype(np.float32) / np.float32(CMP_BLOCK))


def nsa_attention(z, zgate, kv_cmp, B, S, q_col, ks_col, vs_col, kw_col, vw_col, gate_col):
    T = B * S
    G, Dh, hpg = NSA_KV_GROUPS, NSA_HEAD_DIM, NSA_HPG
    GW = G * Dh
    tq = 256
    tk = _pick(S, 512)
    assert S % tq == 0 and WINDOW % tq == 0 and S >= WINDOW + tq and tk % SEL_BLOCK == 0
    NS = kv_cmp.shape[3]
    nb = S // SEL_BLOCK
    nq = S // tq
    bmap_t = block_map(NS, nb).T.astype(BF16)
    assert CMP_BLOCK in (32, 64, 128)
    assert q_col % (hpg * Dh) == 0 and all(c % Dh == 0 for c in (ks_col, vs_col, kw_col, vw_col))
    qb0 = q_col // (hpg * Dh)
    ks_b, vs_b, kw_b, vw_b = (c // Dh for c in (ks_col, vs_col, kw_col, vw_col))
    g_b = gate_col // LANES
    head_f32 = pltpu.VMEM((hpg, tq, Dh), F32)
    cmp_spec = lambda s: pl.BlockSpec((None, None, None, NS, Dh), lambda b, g, i: (s, b, g, 0, 0))
    seq_spec = lambda blk: pl.BlockSpec((S, Dh), lambda b, g, i: (b, blk + g))
    return pl.pallas_call(
        functools.partial(_nsa_kernel, tq=tq, tk=tk, nsel=min(N_SELECT, nb)),
        grid=(B, G, nq),
        in_specs=[pl.BlockSpec((tq, hpg * Dh), lambda b, g, i: (b * nq + i, qb0 + g)),
                  cmp_spec(0), cmp_spec(1),
                  seq_spec(ks_b), seq_spec(vs_b), seq_spec(kw_b), seq_spec(vw_b),
                  pl.BlockSpec((tq, LANES), lambda b, g, i: (b * nq + i, g_b + g)),
                  pl.BlockSpec((nb, NS), lambda b, g, i: (0, 0))],
        out_specs=pl.BlockSpec((tq, hpg * Dh), lambda b, g, i: (b * nq + i, g)),
        out_shape=jax.ShapeDtypeStruct((T, NSA_Q_WIDTH), BF16),
        scratch_shapes=[pltpu.VMEM((S // tk, hpg, tq, tk), F32), head_f32,
                        pltpu.VMEM((hpg, tq, 2 * Dh), F32), head_f32],
        compiler_params=_params("parallel", "parallel", "arbitrary"),
        name="nsa_attention",
    )(z, kv_cmp, kv_cmp, z, z, z, z, zgate, bmap_t)


def _rwkv_prep_kernel(u_ref, up_ref, mu_ref, w0_ref, a0_ref, wb_ref, ab_ref, gb_ref,
                      r_ref, k_ref, v_ref, ld_ref, a_ref, g_ref, *, tm, S):
    i = pl.program_id(0)
    W = RWKV_WIDTH
    u = u_ref[...]
    prev = jnp.where((i * tm) % S == 0, 0.0, up_ref[...][SUBLANES - 1:SUBLANES])
    row = lax.broadcasted_iota(I32, u.shape, 0)
    u_prev = jnp.where(row == 0, prev, pltpu.roll(u, 1, 0))
    um = u + (u_prev - u) * mu_ref[...]
    r_ref[...] = um[:, :W]
    k_ref[...] = um[:, W:2 * W]
    v_ref[...] = um[:, 2 * W:3 * W]
    lora = um[:, RWKV_TAIL:RWKV_TAIL + LANES]
    gl = um[:, RWKV_TAIL + LANES:RWKV_TAIL + LANES + gb_ref.shape[0]]
    y = w0_ref[...] + _dot(jnp.tanh(lora), wb_ref[...], precision=HI)
    ld_ref[...] = -float(np.exp(-0.5)) * jax.nn.sigmoid(y)
    a_ref[...] = jax.nn.sigmoid(a0_ref[...] + _dot(lora, ab_ref[...], precision=HI))
    g_ref[...] = _dot(jax.nn.sigmoid(gl), gb_ref[...], precision=HI)


def rwkv_prep(zr, rwkv_mu, w0, wb, a0, ab, gb, S):
    T = zr.shape[0]
    W, PW = RWKV_WIDTH, RWKV_PAD_WIDTH
    assert RWKV_DECAY_LORA + RWKV_A_LORA == LANES
    glw = -(-RWKV_GATE_LORA // LANES) * LANES
    assert RWKV_TAIL + LANES + glw <= PW
    tm = _pick(S, 256)
    mu = jnp.zeros((1, PW), F32).at[0, :RWKV_IN_WIDTH].set(rwkv_mu)
    wbp = jnp.zeros((LANES, W), F32).at[:RWKV_DECAY_LORA].set(wb)
    abp = jnp.zeros((LANES, W), F32).at[RWKV_DECAY_LORA:].set(ab)
    gbp = jnp.zeros((glw, W), F32).at[:RWKV_GATE_LORA].set(gb)
    row_spec = pl.BlockSpec((1, W), lambda i: (0, 0))
    out_spec = pl.BlockSpec((tm, W), lambda i: (i, 0))
    return pl.pallas_call(
        functools.partial(_rwkv_prep_kernel, tm=tm, S=S),
        grid=(T // tm,),
        in_specs=[pl.BlockSpec((tm, PW), lambda i: (i, 0)),
                  pl.BlockSpec((SUBLANES, PW), lambda i: (jnp.maximum(i * (tm // SUBLANES) - 1, 0), 0)),
                  pl.BlockSpec((1, PW), lambda i: (0, 0)),
                  row_spec, row_spec,
                  pl.BlockSpec((LANES, W), lambda i: (0, 0)),
                  pl.BlockSpec((LANES, W), lambda i: (0, 0)),
                  pl.BlockSpec((glw, W), lambda i: (0, 0))],
        out_specs=[out_spec] * 6,
        out_shape=[jax.ShapeDtypeStruct((T, W), F32)] * 6,
        compiler_params=_params("parallel"),
        name="rwkv_prep",
    )(zr, zr, mu, w0.reshape(1, W), a0.reshape(1, W), wbp, abp, gbp)


def _rwkv_kernel(r_ref, k_ref, v_ref, ld_ref, a_ref, g_ref, kks_ref, ka_ref, rk_ref, lnw_ref, lnb_ref,
                 o_ref, ht_sc, *, pairs):
    C = RWKV_CHUNK
    N = RWKV_HEAD_DIM
    P2 = 2 * C

    @pl.when(pl.program_id(2) == 0)
    def _():
        ht_sc[...] = jnp.zeros(ht_sc.shape, F32)

    row = lax.broadcasted_iota(I32, (P2, LANES), 0)
    col = lax.broadcasted_iota(I32, (P2, LANES), 1)
    same = (row // C) == (col // N)
    sameq = (row // C) == (col // C)
    incl = sameq & (col <= row)
    strict = sameq & (col < row)
    grp = jnp.where((row // N) == (col // N), 1.0, 0.0).astype(BF16)
    tr = lax.broadcasted_iota(I32, (C, C), 0)
    tc = lax.broadcasted_iota(I32, (C, C), 1)
    lower = jnp.where(tc <= tr, 1.0, 0.0).astype(BF16)

    def split(x):
        hi = x.astype(BF16)
        return hi, (x - hi.astype(F32)).astype(BF16)

    def group_sum(x):
        hi, lo = split(x)
        return _dot(hi, grp) + _dot(lo, grp)

    def stack(x):
        return jnp.where(same, jnp.concatenate([x, x], axis=0), 0.0).astype(BF16)

    P = range(pairs)
    sl = [slice(p * LANES, (p + 1) * LANES) for p in P]

    def each(fn):
        return [fn(p) for p in P]

    r = each(lambda p: r_ref[:, sl[p]])
    kraw = each(lambda p: k_ref[:, sl[p]])
    v = each(lambda p: v_ref[:, sl[p]])
    ld = each(lambda p: ld_ref[:, sl[p]])
    a = each(lambda p: a_ref[:, sl[p]])
    kk = each(lambda p: kraw[p] * kks_ref[:, sl[p]])
    nrm = each(lambda p: group_sum(kk[p] * kk[p]))
    kk = each(lambda p: kk[p] / jnp.maximum(jnp.sqrt(nrm[p]), 1e-12))
    k = each(lambda p: kraw[p] * (1.0 + (a[p] - 1.0) * ka_ref[:, sl[p]]))
    ldp = each(lambda p: split(ld[p]))
    cum = each(lambda p: _dot(lower, ldp[p][0]) + _dot(lower, ldp[p][1]))
    tot = each(lambda p: cum[p][C - 1:C])
    beta = each(lambda p: kk[p] * a[p])
    a_s = each(lambda p: stack(-kk[p] * jnp.exp(cum[p] - ld[p])))
    r_s = each(lambda p: stack(r[p] * jnp.exp(cum[p])))
    e_inv = each(lambda p: jnp.exp(-cum[p]))
    e_rem = each(lambda p: jnp.exp(tot[p] - cum[p]))
    bt_s = each(lambda p: stack(beta[p] * e_inv[p]))
    kt_s = each(lambda p: stack(k[p] * e_inv[p]))
    bh_s = each(lambda p: stack(beta[p] * e_rem[p]))
    kh_s = each(lambda p: stack(k[p] * e_rem[p]))
    v_s = each(lambda p: stack(v[p]))
    quad = each(lambda p: _dot_nt(jnp.concatenate([a_s[p], r_s[p]], axis=0),
                                  jnp.concatenate([bt_s[p], kt_s[p]], axis=0)))
    lp = each(lambda p: jnp.where(strict, quad[p][:P2, :P2], 0.0))
    l_ak = each(lambda p: jnp.where(strict, quad[p][:P2, P2:], 0.0).astype(BF16))
    m_rb = each(lambda p: jnp.where(incl, quad[p][P2:, :P2], 0.0).astype(BF16))
    m_rk = each(lambda p: jnp.where(incl, quad[p][P2:, P2:], 0.0).astype(BF16))
    x = each(lambda p: jnp.concatenate([a_s[p].astype(F32), _dot(l_ak[p], v_s[p])], axis=1))
    steps = int(np.log2(C))
    for s in range(steps):
        lpb = each(lambda p: lp[p].astype(BF16))
        x = each(lambda p: x[p] + _dot(lpb[p], x[p].astype(BF16)))
        if s + 1 < steps:
            lp = each(lambda p: _dot(lpb[p], lpb[p]))
    xb = each(lambda p: x[p].astype(BF16))
    ht = each(lambda p: ht_sc[p])
    htb = each(lambda p: ht[p].astype(BF16))
    qo = each(lambda p: _dot(m_rb[p], xb[p]))
    q_eff = each(lambda p: (r_s[p].astype(F32) + qo[p][:, :LANES]).astype(BF16))
    o = each(lambda p: _dot_nt(q_eff[p], htb[p]) + qo[p][:, LANES:] + _dot(m_rk[p], v_s[p]))
    pgt = each(lambda p: _dot_tn(xb[p], bh_s[p]))
    for p in P:
        ht_sc[p] = (ht[p] * jnp.exp(tot[p]) + _dot(htb[p], pgt[p][:LANES].astype(BF16)) + pgt[p][LANES:]
                    + _dot_tn(v_s[p], kh_s[p]))
    o = each(lambda p: o[p][:C] + o[p][C:])
    mean = each(lambda p: group_sum(o[p]) * (1.0 / N))
    d = each(lambda p: o[p] - mean[p])
    var = each(lambda p: group_sum(d[p] * d[p]) * (1.0 / N))
    bonus = each(lambda p: group_sum(r[p] * k[p] * rk_ref[:, sl[p]]))
    for p in P:
        y = d[p] * lax.rsqrt(var[p] + RWKV_LN_EPS) * lnw_ref[:, sl[p]] + lnb_ref[:, sl[p]]
        o_ref[:, sl[p]] = ((y + bonus[p] * v[p]) * g_ref[:, sl[p]]).astype(o_ref.dtype)


def rwkv_mix(r, k, v, ld, a, g, kk_scale, ka, rk, ln_w, ln_b, B, S):
    T, W = r.shape
    C = RWKV_CHUNK
    assert 2 * RWKV_HEAD_DIM == LANES and 2 * C == LANES and S % C == 0
    pairs = 8
    bw = pairs * LANES
    nc = S // C
    seq = pl.BlockSpec((C, bw), lambda b, p, c: (b * nc + c, p))
    par = pl.BlockSpec((1, bw), lambda b, p, c: (0, p))
    return pl.pallas_call(
        functools.partial(_rwkv_kernel, pairs=pairs),
        grid=(B, W // bw, nc),
        in_specs=[seq] * 6 + [par] * 5,
        out_specs=seq,
        out_shape=jax.ShapeDtypeStruct((T, W), BF16),
        scratch_shapes=[pltpu.VMEM((pairs, LANES, LANES), F32)],
        compiler_params=_params("parallel", "parallel", "arbitrary"),
        name="rwkv_mix",
    )(r, k, v, ld, a, g, *(z.reshape(1, W) for z in (kk_scale, ka, rk, ln_w, ln_b)))


def _merge_kernel(ya_ref, yb_ref, yc_ref, pa_ref, pb_ref, pc_ref, ga_ref, gb_ref, gc_ref, o_ref):
    m = ga_ref[...].astype(F32) * _dot(ya_ref[...], pa_ref[...])
    m = m + gb_ref[...].astype(F32) * _dot(yb_ref[...], pb_ref[...])
    m = m + gc_ref[...].astype(F32) * _dot(yc_ref[...], pc_ref[...])
    o_ref[...] = m.astype(o_ref.dtype)


def branch_merge(ya, yb, yc, pa, pb, pc, gm, D):
    T = ya.shape[0]
    tm, tn = _pick(T, 1024), _pick(D, 512)
    nj = D // tn
    ysp = lambda y: pl.BlockSpec((tm, y.shape[1]), lambda i, j: (i, 0))
    psp = lambda p: pl.BlockSpec((p.shape[0], tn), lambda i, j: (0, j))
    gsp = lambda n: pl.BlockSpec((tm, tn), lambda i, j: (i, n * nj + j))
    return pl.pallas_call(
        _merge_kernel,
        grid=(T // tm, nj),
        in_specs=[ysp(ya), ysp(yb), ysp(yc), psp(pa), psp(pb), psp(pc), gsp(0), gsp(1), gsp(2)],
        out_specs=pl.BlockSpec((tm, tn), lambda i, j: (i, j)),
        out_shape=jax.ShapeDtypeStruct((T, D), BF16),
        compiler_params=_params("parallel", "arbitrary"),
        name="branch_merge",
    )(ya, yb, yc, pa, pb, pc, gm, gm, gm)


MOE_DMA_UNROLL = 32


def _moe_kernel(blk_e_ref, asg_ref, h_hbm, wg_ref, bg_ref, wu_ref, bu_ref, wd_ref, bd_ref, y_hbm,
                xbuf, ybuf, gsem, ssem, *, R, n_blk, A):
    i = pl.program_id(0)
    slot = i % 2
    D = ybuf.shape[-1]
    nj = D // LANES

    def per_row(fn, unroll=MOE_DMA_UNROLL):
        def body(r, c):
            fn(r)
            return c
        lax.fori_loop(0, R, body, 0, unroll=unroll)

    def start_gather(step, s, unroll=MOE_DMA_UNROLL):
        def one(r):
            tok = jnp.minimum(asg_ref[step * R + r] >> TOP_K_SHIFT, A // TOP_K - 1)
            pltpu.make_async_copy(h_hbm.at[pl.ds(pl.multiple_of(tok * nj, nj), nj)],
                                  xbuf.at[s, pl.ds(r * nj, nj)], gsem.at[s]).start()
        per_row(one, unroll)

    def start_scatter(step, s):
        def one(r):
            dst = asg_ref[step * R + r]
            col = pl.multiple_of((dst & (TOP_K - 1)) * D, LANES)
            pltpu.make_async_copy(ybuf.at[s, pl.ds(r, 1)],
                                  y_hbm.at[pl.ds(dst >> TOP_K_SHIFT, 1), pl.ds(col, D)], ssem.at[s]).start()
        per_row(one)

    def wait_gather(s):
        pltpu.make_async_copy(h_hbm.at[pl.ds(0, R * nj)], xbuf.at[s], gsem.at[s]).wait()

    def wait_scatter(s):
        pltpu.make_async_copy(ybuf.at[s], y_hbm.at[pl.ds(0, R), pl.ds(0, D)], ssem.at[s]).wait()

    @pl.when(i == 0)
    def _():
        start_gather(0, 0)

    wait_gather(slot)
    x = jnp.concatenate([xbuf[slot, pl.ds(j, R, stride=nj), :] for j in range(nj)], axis=1).astype(BF16)
    start_gather(jnp.minimum(i + 1, n_blk - 1), 1 - slot, unroll=True)
    gt = jnp.minimum(_dot(x, wg_ref[...]) + bg_ref[...], SWIGLU_LIMIT)
    up = jnp.clip(_dot(x, wu_ref[...]) + bu_ref[...], -SWIGLU_LIMIT, SWIGLU_LIMIT)
    act = gt * jax.nn.sigmoid(SWIGLU_ALPHA * gt) * (up + 1.0)
    y = _dot(act.astype(BF16), wd_ref[...]) + bd_ref[...]

    @pl.when(i >= 2)
    def _():
        wait_scatter(slot)

    ybuf[slot] = y
    start_scatter(i, slot)

    @pl.when(i == n_blk - 1)
    def _():
        wait_gather(1 - slot)
        wait_scatter(1 - slot)
        wait_scatter(slot)


def moe_experts(h, blk_e, row_asg, wg, bg, wu, bu, wd, bd, R):
    D = wg.shape[1]
    T = h.shape[0] * LANES // D
    E, _, F = wg.shape
    n_blk = blk_e.shape[0]
    A = T * TOP_K
    assert n_blk >= 2 and (n_blk * R) % TOP_K == 0 and 1 << TOP_K_SHIFT == TOP_K
    wspec = lambda shape: pl.BlockSpec((None,) + shape, lambda i, be, ra: (be[i], 0, 0))
    return pl.pallas_call(
        functools.partial(_moe_kernel, R=R, n_blk=n_blk, A=A),
        grid_spec=pltpu.PrefetchScalarGridSpec(
            num_scalar_prefetch=2,
            grid=(n_blk,),
            in_specs=[pl.BlockSpec(memory_space=pl.ANY),
                      wspec((D, F)), wspec((1, F)), wspec((D, F)), wspec((1, F)),
                      wspec((F, D)), wspec((1, D))],
            out_specs=pl.BlockSpec(memory_space=pl.ANY),
            scratch_shapes=[pltpu.VMEM((2, R * D // LANES, LANES), F32), pltpu.VMEM((2, R, D), F32),
                            pltpu.SemaphoreType.DMA((2,)), pltpu.SemaphoreType.DMA((2,))]),
        out_shape=jax.ShapeDtypeStruct((n_blk * R // TOP_K, TOP_K * D), F32),
        compiler_params=_params("arbitrary"),
        name="moe_experts",
    )(blk_e, row_asg, h, wg, bg.reshape(E, 1, F), wu, bu.reshape(E, 1, F), wd, bd.reshape(E, 1, D))


def _combined(x_ref, y_ref, wt_ref, gate_ref):
    D = x_ref.shape[1]
    wt = wt_ref[...]
    acc = wt[:, 0:1] * y_ref[:, 0:D]
    for k in range(1, TOP_K):
        acc = acc + wt[:, k:k + 1] * y_ref[:, k * D:(k + 1) * D]
    return x_ref[...] + gate_ref[0] * acc


def _combine_next_kernel(x_ref, y_ref, wt_ref, gate_ref, g_ref, sc_ref, sh_ref, o_ref, h_ref):
    x = _combined(x_ref, y_ref, wt_ref, gate_ref)
    o_ref[...] = x
    h_ref[...] = _norm_mod(x, g_ref[...], sc_ref[0], sh_ref[0]).astype(h_ref.dtype)


def _combine_final_kernel(x_ref, y_ref, wt_ref, gate_ref, g_ref, o_ref):
    x = _combined(x_ref, y_ref, wt_ref, gate_ref)
    ms = jnp.mean(x * x, axis=-1, keepdims=True)
    o_ref[...] = x * lax.rsqrt(ms + RMS_EPS) * g_ref[...]


def moe_combine(x, y, wt, gate, S, norm_g, scale=None, shift=None):
    T, D = x.shape
    B = gate.shape[0]
    tm = _pick(S, 128)
    spt = S // tm
    row = pl.BlockSpec((tm, D), lambda i: (i, 0))
    per_batch = pl.BlockSpec((1, 1, D), lambda i: (i // spt, 0, 0))
    in_specs = [row, pl.BlockSpec((tm, TOP_K * D), lambda i: (i, 0)),
                pl.BlockSpec((tm, LANES), lambda i: (i, 0)), per_batch,
                pl.BlockSpec((1, D), lambda i: (0, 0))]
    args = [x, y, wt, gate.reshape(B, 1, D), norm_g.reshape(1, D)]
    if scale is None:
        return pl.pallas_call(
            _combine_final_kernel, grid=(T // tm,), in_specs=in_specs, out_specs=row,
            out_shape=jax.ShapeDtypeStruct((T, D), F32),
            compiler_params=_params("parallel"), name="moe_combine_final")(*args)
    return pl.pallas_call(
        _combine_next_kernel, grid=(T // tm,), in_specs=in_specs + [per_batch, per_batch],
        out_specs=[row, row],
        out_shape=[jax.ShapeDtypeStruct((T, D), F32), jax.ShapeDtypeStruct((T, D), BF16)],
        compiler_params=_params("parallel"), name="moe_combine_next",
    )(*args, scale.reshape(B, 1, D), shift.reshape(B, 1, D))


def route_tables(top_idx, rank, counts, E, R):
    T = top_idx.shape[0]
    A = T * TOP_K
    flat_e = top_idx.reshape(A)
    padded = (counts + R - 1) // R * R
    pends = jnp.cumsum(padded)
    onehot = flat_e[:, None] == jnp.arange(E, dtype=I32)[None, :]
    dest = jnp.sum(jnp.where(onehot, (pends - padded)[None, :], 0), axis=1) + rank.reshape(A)
    n_blk = -(-(A + E * (R - 1)) // R)
    blk_start = jnp.arange(n_blk, dtype=I32)[:, None] * R
    blk_e = jnp.minimum(jnp.sum((pends[None, :] <= blk_start).astype(I32), axis=1), E - 1)
    pads_before = jnp.cumsum(padded - counts) - (padded - counts)
    r = blk_start + jnp.arange(R, dtype=I32)[None, :] - (pends - padded)[blk_e][:, None]
    over = r - counts[blk_e][:, None]
    pad_id = (A + pads_before[blk_e][:, None] + over).reshape(n_blk * R)
    real = jnp.full((n_blk * R,), -1, I32).at[dest].set(jnp.arange(A, dtype=I32))
    row_asg = jnp.where(real < 0, pad_id, real)
    return blk_e, row_asg


def _mixer_weights(w_in, D):
    CW, QW, GW = CONV_WIDTH, NSA_Q_WIDTH, NSA_G_WIDTH
    o_q = 3 * CW
    o_kv = o_q + QW
    o_g = o_kv + 6 * GW
    o_r = o_g + NSA_GATE_WIDTH
    o_m = o_r + RWKV_IN_WIDTH
    per_group = NSA_GATE_WIDTH // NSA_KV_GROUPS
    gate_blocks = [jnp.pad(w_in[:, o_g + g * per_group:o_g + (g + 1) * per_group],
                           ((0, 0), (0, LANES - per_group))) for g in range(NSA_KV_GROUPS)]
    w_rg = jnp.concatenate(
        [jnp.pad(w_in[:, o_r:o_m], ((0, 0), (0, RWKV_PAD_WIDTH - RWKV_IN_WIDTH)))] + gate_blocks,
        axis=1).astype(BF16)
    w_m = w_in[:, o_m:].astype(BF16)
    return w_rg, w_m


def _layer(x, h, mod, next_norm, final_g, B, S, tables, norm_g, w_in, layer, conv_w, cmp_pe, cmp_w1, cmp_w2, rwkv_mu, rwkv_w0,
           rwkv_wb, rwkv_a0, rwkv_ab, rwkv_gb, rwkv_kk, rwkv_ka, rwkv_rk, rwkv_ln_w, rwkv_ln_b,
           w_branch, w_out, router_w, router_b, exp_wg, exp_bg, exp_wu, exp_bu, exp_wd, exp_bd):
    T, D = x.shape
    CW, QW, GW, Dh, G = CONV_WIDTH, NSA_Q_WIDTH, NSA_G_WIDTH, NSA_HEAD_DIM, NSA_KV_GROUPS
    shift1, scale1, gate1, shift2, scale2, gate2 = (mod[:, n] for n in range(N_MOD))

    w_rg, w_m = _mixer_weights(w_in[layer], D)
    z = mm_qkv(h, w_in, layer, tables, S)
    zrg = _mm_call(_mm_plain_kernel, h, w_rg, F32, name="mm_rwkv_gate")
    gm = _mm_call(_mm_sigmoid_kernel, h, w_m, BF16, name="mm_merge_gate")

    y_a = gated_conv(z, conv_w, S)

    ns = S // CMP_STRIDE
    kv0 = 3 * CW + QW
    to_strides = lambda zz: zz.reshape(B, ns, CMP_STRIDE, G, Dh).transpose(0, 3, 1, 2, 4).reshape(
        B, G, ns, CMP_STRIDE * Dh)
    cmp_in = jnp.stack([to_strides(z[:, kv0:kv0 + GW]), to_strides(z[:, kv0 + GW:kv0 + 2 * GW])])
    kv_cmp = compress_kv(cmp_in, cmp_pe, cmp_w1, cmp_w2)
    y_b = nsa_attention(z, zrg, kv_cmp, B, S, q_col=3 * CW, ks_col=kv0 + 2 * GW, vs_col=kv0 + 3 * GW,
                        kw_col=kv0 + 4 * GW, vw_col=kv0 + 5 * GW, gate_col=RWKV_PAD_WIDTH)

    r, k, v, ld, a, g = rwkv_prep(zrg, rwkv_mu, rwkv_w0, rwkv_wb, rwkv_a0, rwkv_ab, rwkv_gb, S)
    y_c = rwkv_mix(r, k, v, ld, a, g, rwkv_kk, rwkv_ka, rwkv_rk.reshape(-1), rwkv_ln_w, rwkv_ln_b, B, S)

    wb16 = w_branch.astype(BF16)
    m = branch_merge(y_a, y_b, y_c, wb16[:CW], wb16[CW:CW + QW], wb16[CW + QW:], gm, D)
    x = mm_residual(m, w_out.astype(BF16), x, gate1, S)

    E = router_w.shape[1]
    h2, top_idx, top_w, counts = norm_router(x, norm_g[1], scale2, shift2, router_w, router_b, S)
    R = _pick(T * TOP_K // E, 256)
    blk_e, row_asg = route_tables(top_idx[:, :TOP_K], top_idx[:, TOP_K:2 * TOP_K], counts[0, :E], E, R)
    y = moe_experts(h2, blk_e, row_asg, exp_wg.astype(BF16), exp_bg, exp_wu.astype(BF16), exp_bu,
                    exp_wd.astype(BF16), exp_bd, R)
    if next_norm is None:
        return moe_combine(x, y, top_w, gate2, S, final_g)
    return moe_combine(x, y, top_w, gate2, S, *next_norm)


def kernel(x, c, ada_w, ada_b, ada_table, norm_g, final_g, w_in, conv_w, cmp_pe, cmp_w1, cmp_w2, rwkv_mu, rwkv_w0, rwkv_wb, rwkv_a0, rwkv_ab, rwkv_gb, rwkv_kk, rwkv_ka, rwkv_rk, rwkv_ln_w, rwkv_ln_b, w_branch, w_out, router_w, router_b, exp_wg, exp_bg, exp_wu, exp_bu, exp_wd, exp_bd):
    B, S, D = x.shape
    depth = w_in.shape[0]
    tables = rope_lane_tables(S)
    mod_all = ada_modulation(c, ada_w, ada_b).reshape(B, N_MOD, D)
    xf = x.reshape(B * S, D)
    mods = [mod_all + ada_table[l] for l in range(depth)]
    w_in16 = w_in.astype(BF16)
    h = norm_modulate(xf, norm_g[0, 0], mods[0][:, 1], mods[0][:, 0], S, BF16)
    for l in range(depth):
        next_norm = None if l + 1 == depth else (norm_g[l + 1, 0], mods[l + 1][:, 1], mods[l + 1][:, 0])
        out = _layer(xf, h, mods[l], next_norm, final_g, B, S, tables, norm_g[l], w_in16, l, conv_w[l],
                     cmp_pe[l], cmp_w1[l], cmp_w2[l], rwkv_mu[l], rwkv_w0[l], rwkv_wb[l], rwkv_a0[l],
                     rwkv_ab[l], rwkv_gb[l], rwkv_kk[l], rwkv_ka[l], rwkv_rk[l], rwkv_ln_w[l],
                     rwkv_ln_b[l], w_branch[l], w_out[l], router_w[l], router_b[l], exp_wg[l], exp_bg[l],
                     exp_wu[l], exp_bu[l], exp_wd[l], exp_bd[l])
        if next_norm is None:
            return out.reshape(B, S, D)
        xf, h = out
```

```python
import functools

import numpy as np
import jax
import jax.numpy as jnp
from jax import lax
from jax.experimental import pallas as pl
from jax.experimental.pallas import tpu as pltpu

F32 = jnp.float32
BF16 = jnp.bfloat16
I32 = jnp.int32
HI = lax.Precision.HIGHEST

N_MOD = 6
RMS_EPS = 1e-6
CONV_WIDTH = 1024
CONV_K = 3
NSA_HEADS = 16
NSA_KV_GROUPS = 4
NSA_HPG = NSA_HEADS // NSA_KV_GROUPS
NSA_HEAD_DIM = 128
CMP_BLOCK = 32
CMP_STRIDE = 16
SEL_BLOCK = 64
N_SELECT = 16
WINDOW = 512
FORCE_SCORE = 1e9
NEG_INF = -1e30
ROPE_THETA = 500000.0
ROPE_DIMS = NSA_HEAD_DIM // 4
NSA_Q_WIDTH = NSA_HEADS * NSA_HEAD_DIM
NSA_G_WIDTH = NSA_KV_GROUPS * NSA_HEAD_DIM
NSA_GATE_WIDTH = 3 * NSA_HEADS
RWKV_HEADS = 16
RWKV_HEAD_DIM = 64
RWKV_WIDTH = RWKV_HEADS * RWKV_HEAD_DIM
RWKV_DECAY_LORA = 64
RWKV_A_LORA = 64
RWKV_GATE_LORA = 160
RWKV_IN_WIDTH = 3 * RWKV_WIDTH + RWKV_DECAY_LORA + RWKV_A_LORA + RWKV_GATE_LORA
RWKV_LN_EPS = 64e-5
RWKV_CHUNK = 64
TOP_K = 4
TOP_K_SHIFT = 2
SWIGLU_ALPHA = 1.702
SWIGLU_LIMIT = 7.0

LANES = 128
SUBLANES = 8
VMEM_BYTES_V7X = 64 * 1024 * 1024
VMEM_LIMIT = VMEM_BYTES_V7X - 12 * 1024 * 1024

RWKV_PAD_WIDTH = -(-RWKV_IN_WIDTH // 512) * 512
RWKV_TAIL = 3 * RWKV_WIDTH


def _params(*sem):
    return pltpu.CompilerParams(dimension_semantics=sem, vmem_limit_bytes=VMEM_LIMIT)


def _pick(n, pref):
    t = min(n, pref)
    while n % t:
        t //= 2
    return t


def _dot(a, b, **kw):
    return jnp.dot(a, b, preferred_element_type=F32, **kw)


def _dot_nt(a, b, **kw):
    return lax.dot_general(a, b, (((1,), (1,)), ((), ())), preferred_element_type=F32, **kw)


def _dot_tn(a, b, **kw):
    return lax.dot_general(a, b, (((0,), (0,)), ((), ())), preferred_element_type=F32, **kw)


def _ada_kernel(c_ref, w_ref, b_ref, o_ref):
    c = c_ref[...]
    sc = c * jax.nn.sigmoid(c)
    o_ref[...] = _dot(sc, w_ref[...], precision=HI) + b_ref[...]


def ada_modulation(c, ada_w, ada_b):
    B, D = c.shape
    N = ada_w.shape[1]
    rows = -(-B // SUBLANES) * SUBLANES
    cp = jnp.zeros((rows, D), F32).at[:B].set(c)
    tn = _pick(N, 512)
    out = pl.pallas_call(
        _ada_kernel,
        grid=(N // tn,),
        in_specs=[pl.BlockSpec((rows, D), lambda j: (0, 0)),
                  pl.BlockSpec((D, tn), lambda j: (0, j)),
                  pl.BlockSpec((1, tn), lambda j: (0, j))],
        out_specs=pl.BlockSpec((rows, tn), lambda j: (0, j)),
        out_shape=jax.ShapeDtypeStruct((rows, N), F32),
        compiler_params=_params("parallel"),
        name="ada_mod",
    )(cp, ada_w, ada_b.reshape(1, N))
    return out[:B]


def _norm_mod(x, g, sc, sh):
    ms = jnp.mean(x * x, axis=-1, keepdims=True)
    y = x * lax.rsqrt(ms + RMS_EPS) * g
    return y * (1.0 + sc) + sh


def _norm_mod_kernel(x_ref, g_ref, sc_ref, sh_ref, o_ref):
    h = _norm_mod(x_ref[...], g_ref[...], sc_ref[0], sh_ref[0])
    o_ref[...] = h.astype(o_ref.dtype)


def norm_modulate(x, g, scale, shift, S, out_dtype):
    T, D = x.shape
    B = scale.shape[0]
    tm = _pick(S, 256)
    spt = S // tm
    return pl.pallas_call(
        _norm_mod_kernel,
        grid=(T // tm,),
        in_specs=[pl.BlockSpec((tm, D), lambda i: (i, 0)),
                  pl.BlockSpec((1, D), lambda i: (0, 0)),
                  pl.BlockSpec((1, 1, D), lambda i: (i // spt, 0, 0)),
                  pl.BlockSpec((1, 1, D), lambda i: (i // spt, 0, 0))],
        out_specs=pl.BlockSpec((tm, D), lambda i: (i, 0)),
        out_shape=jax.ShapeDtypeStruct((T, D), out_dtype),
        compiler_params=_params("parallel"),
        name="norm_mod",
    )(x, g.reshape(1, D), scale.reshape(B, 1, D), shift.reshape(B, 1, D))


def _norm_router_kernel(x_ref, g_ref, sc_ref, sh_ref, rw_ref, rb_ref, h_ref, idx_ref, wt_ref, cnt_ref,
                        base_sc):
    @pl.when(pl.program_id(0) == 0)
    def _():
        base_sc[...] = jnp.zeros(base_sc.shape, F32)

    h = _norm_mod(x_ref[...], g_ref[...], sc_ref[0], sh_ref[0])
    h_ref[...] = h
    logits = _dot(h, rw_ref[...], precision=HI) + rb_ref[...]
    tm = logits.shape[0]
    lane = lax.broadcasted_iota(I32, logits.shape, 1)
    lanef = lane.astype(F32)
    idx_out = jnp.zeros(logits.shape, I32)
    val_out = jnp.zeros(logits.shape, F32)
    v0 = None
    hits = []
    for k in range(TOP_K):
        mx = jnp.max(logits, axis=1, keepdims=True)
        first = jnp.min(jnp.where(logits == mx, lanef, float(LANES)), axis=1, keepdims=True)
        hit = lanef == first
        hits.append(jnp.where(hit, 1.0, 0.0))
        if k == 0:
            v0 = mx
        idx_out = jnp.where(lane == k, first.astype(I32), idx_out)
        val_out = jnp.where(lane == k, jnp.exp(mx - v0), val_out)
        logits = jnp.where(hit, -3e38, logits)
    wt_ref[...] = val_out / jnp.sum(val_out, axis=1, keepdims=True)
    used = hits[0]
    for k in range(1, TOP_K):
        used = used + hits[k]
    earlier = jnp.where(lax.broadcasted_iota(I32, (tm, tm), 1) < lax.broadcasted_iota(I32, (tm, tm), 0),
                        1.0, 0.0).astype(BF16)
    before = base_sc[0:1] + _dot(earlier, used.astype(BF16))
    for k in range(TOP_K):
        rank = jnp.sum(hits[k] * before, axis=1, keepdims=True)
        idx_out = jnp.where(lane == TOP_K + k, rank.astype(I32), idx_out)
    idx_ref[...] = idx_out
    total = base_sc[0:1] + jnp.sum(used, axis=0, keepdims=True)
    base_sc[...] = jnp.broadcast_to(total, base_sc.shape)
    cnt_ref[...] = jnp.broadcast_to(total, cnt_ref.shape).astype(I32)


def norm_router(x, g, scale, shift, router_w, router_b, S):
    T, D = x.shape
    B = scale.shape[0]
    E = router_w.shape[1]
    assert E <= LANES
    rw = jnp.zeros((D, LANES), F32).at[:, :E].set(router_w)
    rb = jnp.full((1, LANES), NEG_INF, F32).at[0, :E].set(router_b)
    tm = _pick(S, 256)
    spt = S // tm
    return pl.pallas_call(
        _norm_router_kernel,
        grid=(T // tm,),
        in_specs=[pl.BlockSpec((tm, D), lambda i: (i, 0)),
                  pl.BlockSpec((1, D), lambda i: (0, 0)),
                  pl.BlockSpec((1, 1, D), lambda i: (i // spt, 0, 0)),
                  pl.BlockSpec((1, 1, D), lambda i: (i // spt, 0, 0)),
                  pl.BlockSpec((D, LANES), lambda i: (0, 0)),
                  pl.BlockSpec((1, LANES), lambda i: (0, 0))],
        out_specs=[pl.BlockSpec((tm, D), lambda i: (i, 0)),
                   pl.BlockSpec((tm, LANES), lambda i: (i, 0)),
                   pl.BlockSpec((tm, LANES), lambda i: (i, 0)),
                   pl.BlockSpec((SUBLANES, LANES), lambda i: (0, 0))],
        out_shape=[jax.ShapeDtypeStruct((T, D), F32),
                   jax.ShapeDtypeStruct((T, LANES), I32),
                   jax.ShapeDtypeStruct((T, LANES), F32),
                   jax.ShapeDtypeStruct((SUBLANES, LANES), I32)],
        scratch_shapes=[pltpu.VMEM((SUBLANES, LANES), F32)],
        compiler_params=_params("arbitrary"),
        name="norm_router",
    )(x, g.reshape(1, D), scale.reshape(B, 1, D), shift.reshape(B, 1, D), rw, rb)


def _rope_lanes(x, cf, sa, sb):
    half = ROPE_DIMS // 2
    return x * cf + pltpu.roll(x, LANES - half, 1) * sa + pltpu.roll(x, half, 1) * sb


def _mm_plain_kernel(a_ref, b_ref, o_ref):
    o_ref[...] = _dot(a_ref[...], b_ref[...]).astype(o_ref.dtype)


def _mm_sigmoid_kernel(a_ref, b_ref, o_ref):
    o_ref[...] = jax.nn.sigmoid(_dot(a_ref[...], b_ref[...])).astype(o_ref.dtype)


def _mm_qkv_kernel(a_ref, b_ref, cf_ref, sa_ref, sb_ref, o_ref, *, q_lo, q_hi, kv_lo):
    j = pl.program_id(1)
    is_q = (j >= q_lo) & (j < q_hi)
    is_key = (j >= kv_lo) & ((j - kv_lo) % 2 == 0)
    acc = _dot(a_ref[...], b_ref[...])

    @pl.when(is_q | is_key)
    def _():
        fac = jnp.where(is_q, NSA_HEAD_DIM ** -0.5, 1.0)
        cf, sa, sb = cf_ref[...], sa_ref[...], sb_ref[...]
        for n in range(acc.shape[1] // LANES):
            sl = slice(n * LANES, (n + 1) * LANES)
            o_ref[:, sl] = _rope_lanes(acc[:, sl] * fac, cf, sa, sb).astype(o_ref.dtype)

    @pl.when(jnp.logical_not(is_q | is_key))
    def _():
        o_ref[...] = acc.astype(o_ref.dtype)


def _mm_residual_kernel(a_ref, b_ref, x_ref, gate_ref, o_ref):
    o_ref[...] = x_ref[...] + gate_ref[0] * _dot(a_ref[...], b_ref[...])


def _mm_call(kernel, a, b, out_dtype, extra=(), extra_specs=(), tm_pref=1024, tn_pref=512, name="mm"):
    M, K = a.shape
    N = b.shape[1]
    tm, tn = _pick(M, tm_pref), _pick(N, tn_pref)
    return pl.pallas_call(
        kernel,
        grid=(M // tm, N // tn),
        in_specs=[pl.BlockSpec((tm, K), lambda i, j: (i, 0)),
                  pl.BlockSpec((K, tn), lambda i, j: (0, j))] + [s(tm, tn) for s in extra_specs],
        out_specs=pl.BlockSpec((tm, tn), lambda i, j: (i, j)),
        out_shape=jax.ShapeDtypeStruct((M, N), out_dtype),
        compiler_params=_params("parallel", "arbitrary"),
        name=name,
    )(a, b, *extra)


def rope_lane_tables(S):
    half = ROPE_DIMS // 2
    inv = ROPE_THETA ** (-jnp.arange(0, ROPE_DIMS, 2, dtype=F32) / ROPE_DIMS)
    ang = jnp.arange(S, dtype=F32)[:, None] * inv[None, :]
    cos, sin = jnp.cos(ang), jnp.sin(ang)
    zeros = jnp.zeros((S, LANES - ROPE_DIMS), F32)
    zh = jnp.zeros((S, half), F32)
    cf = jnp.concatenate([cos, cos, zeros + 1.0], axis=1)
    sa = jnp.concatenate([-sin, zh, zeros], axis=1)
    sb = jnp.concatenate([zh, sin, zeros], axis=1)
    return cf, sa, sb


def mm_qkv(a, b, tables, S):
    T = a.shape[0]
    tm = _pick(min(T, S), 1024)
    spt = S // tm
    spec = lambda tm_, tn_: pl.BlockSpec((tm_, LANES), lambda i, j: (i % spt, 0))
    tn = NSA_G_WIDTH
    q0, kv0 = 3 * CONV_WIDTH, 3 * CONV_WIDTH + NSA_Q_WIDTH
    assert q0 % tn == 0 and kv0 % tn == 0 and b.shape[1] == kv0 + 6 * tn
    kern = functools.partial(_mm_qkv_kernel, q_lo=q0 // tn, q_hi=kv0 // tn, kv_lo=kv0 // tn)
    return _mm_call(kern, a, b, BF16, extra=tables, extra_specs=(spec,) * 3, tm_pref=tm, tn_pref=tn,
                    name="mm_qkv")


def mm_residual(a, b, x, gate, S):
    T = a.shape[0]
    B, D = gate.shape
    tm = _pick(min(T, S), 1024)
    spt = S // tm
    xspec = lambda tm_, tn_: pl.BlockSpec((tm_, tn_), lambda i, j: (i, j))
    gspec = lambda tm_, tn_: pl.BlockSpec((1, 1, tn_), lambda i, j: (i // spt, 0, j))
    return _mm_call(_mm_residual_kernel, a, b, F32, extra=(x, gate.reshape(B, 1, D)),
                    extra_specs=(xspec, gspec), tm_pref=tm, name="mm_residual")


def _conv_kernel(z_ref, zp_ref, w_ref, o_ref, *, tm, S):
    i = pl.program_id(0)
    W = CONV_WIDTH
    z = z_ref[...].astype(F32)
    cb, u = z[:, :W], z[:, W:2 * W] * z[:, 2 * W:]
    zp = zp_ref[...].astype(F32)
    up = zp[:, W:2 * W] * zp[:, 2 * W:]
    up = jnp.where((i * tm) % S == 0, 0.0, up)
    row = lax.broadcasted_iota(I32, u.shape, 0)
    last, prev = up[SUBLANES - 1:SUBLANES], up[SUBLANES - 2:SUBLANES - 1]
    u1 = jnp.where(row == 0, last, pltpu.roll(u, 1, 0))
    u2 = jnp.where(row == 0, prev, jnp.where(row == 1, last, pltpu.roll(u, 2, 0)))
    w = w_ref[...]
    o_ref[...] = (cb * (w[0:1] * u2 + w[1:2] * u1 + w[2:3] * u)).astype(o_ref.dtype)


def gated_conv(z, conv_w, S):
    T = z.shape[0]
    W = CONV_WIDTH
    assert CONV_K == 3
    tm = _pick(S, 512)
    wp = jnp.zeros((SUBLANES, W), F32).at[:CONV_K].set(conv_w)
    return pl.pallas_call(
        functools.partial(_conv_kernel, tm=tm, S=S),
        grid=(T // tm,),
        in_specs=[pl.BlockSpec((tm, 3 * W), lambda i: (i, 0)),
                  pl.BlockSpec((SUBLANES, 3 * W), lambda i: (jnp.maximum(i * (tm // SUBLANES) - 1, 0), 0)),
                  pl.BlockSpec((SUBLANES, W), lambda i: (0, 0))],
        out_specs=pl.BlockSpec((tm, W), lambda i: (i, 0)),
        out_shape=jax.ShapeDtypeStruct((T, W), BF16),
        compiler_params=_params("parallel"),
        name="gated_conv",
    )(z, z, wp)


def _compress_kernel(x_ref, w1_ref, pe_ref, w2_ref, o_ref):
    x = x_ref[...]
    w1 = w1_ref[...]
    half = x.shape[1]
    ns = x.shape[0]
    first = _dot(x, w1[:half].astype(BF16))
    second = _dot(x, w1[half:].astype(BF16))
    bias = _dot(pe_ref[...], w1, precision=HI)[0:1]
    hid = first + pltpu.roll(second, ns - 1, 0) + bias
    act = jax.nn.gelu(hid, approximate=True)
    o_ref[...] = _dot(act.astype(BF16), w2_ref[...].astype(BF16)).astype(o_ref.dtype)


def compress_kv(x, cmp_pe, cmp_w1, cmp_w2):
    _, B, G, NS, W = x.shape
    assert CMP_BLOCK == 2 * CMP_STRIDE
    Dh = NSA_HEAD_DIM
    hid = cmp_w1.shape[-1]
    pe = jnp.broadcast_to(cmp_pe.reshape(2, 1, CMP_BLOCK * Dh), (2, SUBLANES, CMP_BLOCK * Dh))
    return pl.pallas_call(
        _compress_kernel,
        grid=(2, B, G),
        in_specs=[pl.BlockSpec((None, None, None, NS, W), lambda s, b, g: (s, b, g, 0, 0)),
                  pl.BlockSpec((None, CMP_BLOCK * Dh, hid), lambda s, b, g: (s, 0, 0)),
                  pl.BlockSpec((None, SUBLANES, CMP_BLOCK * Dh), lambda s, b, g: (s, 0, 0)),
                  pl.BlockSpec((None, hid, Dh), lambda s, b, g: (s, 0, 0))],
        out_specs=pl.BlockSpec((None, None, None, NS, Dh), lambda s, b, g: (s, b, g, 0, 0)),
        out_shape=jax.ShapeDtypeStruct((2, B, G, NS, Dh), BF16),
        compiler_params=_params("parallel", "parallel", "parallel"),
        name="nsa_compress",
    )(x, cmp_w1, pe, cmp_w2)


def _nsa_kernel(q_ref, kc_ref, vc_ref, ks_ref, vs_ref, kw_ref, vw_ref, zg_ref, bmapt_ref, o_ref,
                s_sc, m_sc, acc_sc, out_sc, *, tq, tk, nsel):
    t0 = pl.program_id(2) * tq
    Dh = NSA_HEAD_DIM
    hpg = NSA_HPG
    nch = tk // LANES
    heads = range(hpg)
    trow = t0 + lax.broadcasted_iota(I32, (tq, 1), 0)
    gates = jax.nn.sigmoid(zg_ref[...])

    def q(h):
        return q_ref[:, h * Dh:(h + 1) * Dh]

    def gate(h, branch):
        return gates[:, 3 * h + branch:3 * h + branch + 1]

    kc, vc = kc_ref[...], vc_ref[...]
    ncb = kc.shape[0]
    blk_end = lax.broadcasted_iota(I32, (1, ncb), 1) * CMP_STRIDE + (CMP_BLOCK - 1)
    valid = blk_end <= trow
    bias_c = jnp.where(valid, 0.0, NEG_INF)
    validf = jnp.where(valid, 1.0, 0.0)
    psum = None
    for h in heads:
        sm = _dot_nt(q(h), kc) + bias_c
        p = jnp.exp(sm - jnp.max(sm, axis=1, keepdims=True)) * validf
        pc = p / jnp.maximum(jnp.sum(p, axis=1, keepdims=True), 1e-30)
        out_sc[h] = gate(h, 0) * _dot(pc.astype(BF16), vc)
        psum = pc if psum is None else psum + pc

    bmt = bmapt_ref[...]
    hi = psum.astype(BF16)
    rest = psum - hi.astype(F32)
    mid = rest.astype(BF16)
    lo = (rest - mid.astype(F32)).astype(BF16)
    imp = _dot_nt(bmt, hi) + _dot_nt(bmt, mid) + _dot_nt(bmt, lo)
    nb = imp.shape[0]
    jb = lax.broadcasted_iota(I32, (nb, tq), 0)
    cur = (t0 + lax.broadcasted_iota(I32, (1, tq), 1)) // SEL_BLOCK
    forced = (jb == 0) | (jb == cur) | (jb == cur - 1)
    imp = jnp.where(forced, FORCE_SCORE, imp)
    imp = jnp.where(jb <= cur, imp, NEG_INF)
    rank = jnp.zeros((nb, tq), F32)
    for i in range(nb):
        ri = imp[i:i + 1, :]
        rank = rank + jnp.where((ri > imp) | ((ri == imp) & (jb > i)), 1.0, 0.0)
    sel_t = jnp.where(rank < nsel, 1.0, 0.0)

    bpt = tk // SEL_BLOCK
    nkt = (t0 + tq + tk - 1) // tk
    m_sc[...] = jnp.full(m_sc.shape, NEG_INF, F32)
    acc_sc[...] = jnp.zeros(acc_sc.shape, F32)

    def chunks(a):
        return [a[:, c * LANES:(c + 1) * LANES] for c in range(nch)]

    def score_sweep(kt, carry):
        k = ks_ref[pl.ds(pl.multiple_of(kt * tk, tk), tk), :]
        expand = jnp.where(lax.broadcasted_iota(I32, (nb, tk), 0)
                           == kt * bpt + lax.broadcasted_iota(I32, (nb, tk), 1) // SEL_BLOCK, 1.0, 0.0)
        chosen = _dot_tn(sel_t, expand)
        kpos = kt * tk + lax.broadcasted_iota(I32, (1, tk), 1)
        bias = jnp.where((chosen > 0.5) & (kpos <= trow), 0.0, NEG_INF)
        for h in heads:
            s = _dot_nt(q(h), k) + bias
            s_sc[kt, h] = s
            m = m_sc[h]
            for sc in chunks(s):
                m = jnp.maximum(m, sc)
            m_sc[h] = m
        return carry

    lax.fori_loop(0, nkt, score_sweep, 0)
    for h in heads:
        m_sc[h] = jnp.broadcast_to(jnp.max(m_sc[h], axis=1, keepdims=True), (tq, LANES))
    ones = jnp.ones((tk, LANES), BF16)

    def exp_sweep(kt, carry):
        v1 = jnp.concatenate([vs_ref[pl.ds(pl.multiple_of(kt * tk, tk), tk), :], ones], axis=1)
        for h in heads:
            m = m_sc[h]
            p = jnp.concatenate([jnp.exp(sc - m) for sc in chunks(s_sc[kt, h])], axis=1)
            acc_sc[h] = acc_sc[h] + _dot(p.astype(BF16), v1)
        return carry

    lax.fori_loop(0, nkt, exp_sweep, 0)
    for h in heads:
        acc = acc_sc[h]
        out_sc[h] = out_sc[h] + gate(h, 1) * (acc[:, :Dh] / acc[:, Dh:Dh + 1])

    span = WINDOW + tq
    w0 = pl.multiple_of(jnp.maximum(t0 - WINDOW, 0), tq)
    kwv = kw_ref[pl.ds(w0, span), :]
    vwv = vw_ref[pl.ds(w0, span), :]
    diff = trow - (w0 + lax.broadcasted_iota(I32, (1, span), 1))
    bias_w = jnp.where((diff >= 0) & (diff < WINDOW), 0.0, NEG_INF)
    for h in heads:
        sw = _dot_nt(q(h), kwv) + bias_w
        pw = jnp.exp(sw - jnp.max(sw, axis=1, keepdims=True))
        o_w = _dot(pw.astype(BF16), vwv) / jnp.sum(pw, axis=1, keepdims=True)
        o_ref[:, h * Dh:(h + 1) * Dh] = (out_sc[h] + gate(h, 2) * o_w).astype(o_ref.dtype)


def block_map(nc, nb):
    i = np.arange(nc)[:, None] * CMP_STRIDE
    j = np.arange(nb)[None, :] * SEL_BLOCK
    inter = np.clip(np.minimum(i + CMP_BLOCK, j + SEL_BLOCK) - np.maximum(i, j), 0, None)
    return jnp.asarray(inter.astype(np.float32) / np.float32(CMP_BLOCK))


def nsa_attention(z, zgate, kv_cmp, B, S, q_col, ks_col, vs_col, kw_col, vw_col, gate_col):
    T = B * S
    G, Dh, hpg = NSA_KV_GROUPS, NSA_HEAD_DIM, NSA_HPG
    GW = G * Dh
    tq = 256
    tk = _pick(S, 512)
    assert S % tq == 0 and WINDOW % tq == 0 and S >= WINDOW + tq and tk % SEL_BLOCK == 0
    NS = kv_cmp.shape[3]
    nb = S // SEL_BLOCK
    nq = S // tq
    bmap_t = block_map(NS, nb).T.astype(BF16)
    assert CMP_BLOCK in (32, 64, 128)
    assert q_col % (hpg * Dh) == 0 and all(c % Dh == 0 for c in (ks_col, vs_col, kw_col, vw_col))
    qb0 = q_col // (hpg * Dh)
    ks_b, vs_b, kw_b, vw_b = (c // Dh for c in (ks_col, vs_col, kw_col, vw_col))
    g_b = gate_col // LANES
    head_f32 = pltpu.VMEM((hpg, tq, Dh), F32)
    cmp_spec = lambda s: pl.BlockSpec((None, None, None, NS, Dh), lambda b, g, i: (s, b, g, 0, 0))
    seq_spec = lambda blk: pl.BlockSpec((S, Dh), lambda b, g, i: (b, blk + g))
    return pl.pallas_call(
        functools.partial(_nsa_kernel, tq=tq, tk=tk, nsel=min(N_SELECT, nb)),
        grid=(B, G, nq),
        in_specs=[pl.BlockSpec((tq, hpg * Dh), lambda b, g, i: (b * nq + i, qb0 + g)),
                  cmp_spec(0), cmp_spec(1),
                  seq_spec(ks_b), seq_spec(vs_b), seq_spec(kw_b), seq_spec(vw_b),
                  pl.BlockSpec((tq, LANES), lambda b, g, i: (b * nq + i, g_b + g)),
                  pl.BlockSpec((nb, NS), lambda b, g, i: (0, 0))],
        out_specs=pl.BlockSpec((tq, hpg * Dh), lambda b, g, i: (b * nq + i, g)),
        out_shape=jax.ShapeDtypeStruct((T, NSA_Q_WIDTH), BF16),
        scratch_shapes=[pltpu.VMEM((S // tk, hpg, tq, tk), F32), head_f32,
                        pltpu.VMEM((hpg, tq, 2 * Dh), F32), head_f32],
        compiler_params=_params("parallel", "parallel", "arbitrary"),
        name="nsa_attention",
    )(z, kv_cmp, kv_cmp, z, z, z, z, zgate, bmap_t)


def _rwkv_prep_kernel(u_ref, up_ref, mu_ref, w0_ref, a0_ref, wb_ref, ab_ref, gb_ref,
                      r_ref, k_ref, v_ref, ld_ref, a_ref, g_ref, *, tm, S):
    i = pl.program_id(0)
    W = RWKV_WIDTH
    u = u_ref[...]
    prev = jnp.where((i * tm) % S == 0, 0.0, up_ref[...][SUBLANES - 1:SUBLANES])
    row = lax.broadcasted_iota(I32, u.shape, 0)
    u_prev = jnp.where(row == 0, prev, pltpu.roll(u, 1, 0))
    um = u + (u_prev - u) * mu_ref[...]
    r_ref[...] = um[:, :W]
    k_ref[...] = um[:, W:2 * W]
    v_ref[...] = um[:, 2 * W:3 * W]
    lora = um[:, RWKV_TAIL:RWKV_TAIL + LANES]
    gl = um[:, RWKV_TAIL + LANES:RWKV_TAIL + LANES + gb_ref.shape[0]]
    y = w0_ref[...] + _dot(jnp.tanh(lora), wb_ref[...], precision=HI)
    ld_ref[...] = -float(np.exp(-0.5)) * jax.nn.sigmoid(y)
    a_ref[...] = jax.nn.sigmoid(a0_ref[...] + _dot(lora, ab_ref[...], precision=HI))
    g_ref[...] = _dot(jax.nn.sigmoid(gl), gb_ref[...], precision=HI)


def rwkv_prep(zr, rwkv_mu, w0, wb, a0, ab, gb, S):
    T = zr.shape[0]
    W, PW = RWKV_WIDTH, RWKV_PAD_WIDTH
    assert RWKV_DECAY_LORA + RWKV_A_LORA == LANES
    glw = -(-RWKV_GATE_LORA // LANES) * LANES
    assert RWKV_TAIL + LANES + glw <= PW
    tm = _pick(S, 256)
    mu = jnp.zeros((1, PW), F32).at[0, :RWKV_IN_WIDTH].set(rwkv_mu)
    wbp = jnp.zeros((LANES, W), F32).at[:RWKV_DECAY_LORA].set(wb)
    abp = jnp.zeros((LANES, W), F32).at[RWKV_DECAY_LORA:].set(ab)
    gbp = jnp.zeros((glw, W), F32).at[:RWKV_GATE_LORA].set(gb)
    row_spec = pl.BlockSpec((1, W), lambda i: (0, 0))
    out_spec = pl.BlockSpec((tm, W), lambda i: (i, 0))
    return pl.pallas_call(
        functools.partial(_rwkv_prep_kernel, tm=tm, S=S),
        grid=(T // tm,),
        in_specs=[pl.BlockSpec((tm, PW), lambda i: (i, 0)),
                  pl.BlockSpec((SUBLANES, PW), lambda i: (jnp.maximum(i * (tm // SUBLANES) - 1, 0), 0)),
                  pl.BlockSpec((1, PW), lambda i: (0, 0)),
                  row_spec, row_spec,
                  pl.BlockSpec((LANES, W), lambda i: (0, 0)),
                  pl.BlockSpec((LANES, W), lambda i: (0, 0)),
                  pl.BlockSpec((glw, W), lambda i: (0, 0))],
        out_specs=[out_spec] * 6,
        out_shape=[jax.ShapeDtypeStruct((T, W), F32)] * 6,
        compiler_params=_params("parallel"),
        name="rwkv_prep",
    )(zr, zr, mu, w0.reshape(1, W), a0.reshape(1, W), wbp, abp, gbp)


def _rwkv_kernel(r_ref, k_ref, v_ref, ld_ref, a_ref, g_ref, kks_ref, ka_ref, rk_ref, lnw_ref, lnb_ref,
                 o_ref, ht_sc, *, pairs):
    C = RWKV_CHUNK
    N = RWKV_HEAD_DIM
    P2 = 2 * C

    @pl.when(pl.program_id(2) == 0)
    def _():
        ht_sc[...] = jnp.zeros(ht_sc.shape, F32)

    row = lax.broadcasted_iota(I32, (P2, LANES), 0)
    col = lax.broadcasted_iota(I32, (P2, LANES), 1)
    same = (row // C) == (col // N)
    sameq = (row // C) == (col // C)
    incl = sameq & (col <= row)
    strict = sameq & (col < row)
    grp = jnp.where((row // N) == (col // N), 1.0, 0.0).astype(BF16)
    tr = lax.broadcasted_iota(I32, (C, C), 0)
    tc = lax.broadcasted_iota(I32, (C, C), 1)
    lower = jnp.where(tc <= tr, 1.0, 0.0).astype(BF16)

    def split(x):
        hi = x.astype(BF16)
        return hi, (x - hi.astype(F32)).astype(BF16)

    def group_sum(x):
        hi, lo = split(x)
        return _dot(hi, grp) + _dot(lo, grp)

    def stack(x):
        return jnp.where(same, jnp.concatenate([x, x], axis=0), 0.0).astype(BF16)

    P = range(pairs)
    sl = [slice(p * LANES, (p + 1) * LANES) for p in P]

    def each(fn):
        return [fn(p) for p in P]

    r = each(lambda p: r_ref[:, sl[p]])
    kraw = each(lambda p: k_ref[:, sl[p]])
    v = each(lambda p: v_ref[:, sl[p]])
    ld = each(lambda p: ld_ref[:, sl[p]])
    a = each(lambda p: a_ref[:, sl[p]])
    kk = each(lambda p: kraw[p] * kks_ref[:, sl[p]])
    nrm = each(lambda p: group_sum(kk[p] * kk[p]))
    kk = each(lambda p: kk[p] / jnp.maximum(jnp.sqrt(nrm[p]), 1e-12))
    k = each(lambda p: kraw[p] * (1.0 + (a[p] - 1.0) * ka_ref[:, sl[p]]))
    ldp = each(lambda p: split(ld[p]))
    cum = each(lambda p: _dot(lower, ldp[p][0]) + _dot(lower, ldp[p][1]))
    tot = each(lambda p: cum[p][C - 1:C])
    beta = each(lambda p: kk[p] * a[p])
    a_s = each(lambda p: stack(-kk[p] * jnp.exp(cum[p] - ld[p])))
    r_s = each(lambda p: stack(r[p] * jnp.exp(cum[p])))
    e_inv = each(lambda p: jnp.exp(-cum[p]))
    e_rem = each(lambda p: jnp.exp(tot[p] - cum[p]))
    bt_s = each(lambda p: stack(beta[p] * e_inv[p]))
    kt_s = each(lambda p: stack(k[p] * e_inv[p]))
    bh_s = each(lambda p: stack(beta[p] * e_rem[p]))
    kh_s = each(lambda p: stack(k[p] * e_rem[p]))
    v_s = each(lambda p: stack(v[p]))
    quad = each(lambda p: _dot_nt(jnp.concatenate([a_s[p], r_s[p]], axis=0),
                                  jnp.concatenate([bt_s[p], kt_s[p]], axis=0)))
    lp = each(lambda p: jnp.where(strict, quad[p][:P2, :P2], 0.0))
    l_ak = each(lambda p: jnp.where(strict, quad[p][:P2, P2:], 0.0).astype(BF16))
    m_rb = each(lambda p: jnp.where(incl, quad[p][P2:, :P2], 0.0).astype(BF16))
    m_rk = each(lambda p: jnp.where(incl, quad[p][P2:, P2:], 0.0).astype(BF16))
    x = each(lambda p: jnp.concatenate([a_s[p].astype(F32), _dot(l_ak[p], v_s[p])], axis=1))
    steps = int(np.log2(C))
    for s in range(steps):
        lpb = each(lambda p: lp[p].astype(BF16))
        x = each(lambda p: x[p] + _dot(lpb[p], x[p].astype(BF16)))
        if s + 1 < steps:
            lp = each(lambda p: _dot(lpb[p], lpb[p]))
    xb = each(lambda p: x[p].astype(BF16))
    ht = each(lambda p: ht_sc[p])
    htb = each(lambda p: ht[p].astype(BF16))
    qo = each(lambda p: _dot(m_rb[p], xb[p]))
    q_eff = each(lambda p: (r_s[p].astype(F32) + qo[p][:, :LANES]).astype(BF16))
    o = each(lambda p: _dot_nt(q_eff[p], htb[p]) + qo[p][:, LANES:] + _dot(m_rk[p], v_s[p]))
    pgt = each(lambda p: _dot_tn(xb[p], bh_s[p]))
    for p in P:
        ht_sc[p] = (ht[p] * jnp.exp(tot[p]) + _dot(htb[p], pgt[p][:LANES].astype(BF16)) + pgt[p][LANES:]
                    + _dot_tn(v_s[p], kh_s[p]))
    o = each(lambda p: o[p][:C] + o[p][C:])
    mean = each(lambda p: group_sum(o[p]) * (1.0 / N))
    d = each(lambda p: o[p] - mean[p])
    var = each(lambda p: group_sum(d[p] * d[p]) * (1.0 / N))
    bonus = each(lambda p: group_sum(r[p] * k[p] * rk_ref[:, sl[p]]))
    for p in P:
        y = d[p] * lax.rsqrt(var[p] + RWKV_LN_EPS) * lnw_ref[:, sl[p]] + lnb_ref[:, sl[p]]
        o_ref[:, sl[p]] = ((y + bonus[p] * v[p]) * g_ref[:, sl[p]]).astype(o_ref.dtype)


def rwkv_mix(r, k, v, ld, a, g, kk_scale, ka, rk, ln_w, ln_b, B, S):
    T, W = r.shape
    C = RWKV_CHUNK
    assert 2 * RWKV_HEAD_DIM == LANES and 2 * C == LANES and S % C == 0
    pairs = 8
    bw = pairs * LANES
    nc = S // C
    seq = pl.BlockSpec((C, bw), lambda b, p, c: (b * nc + c, p))
    par = pl.BlockSpec((1, bw), lambda b, p, c: (0, p))
    return pl.pallas_call(
        functools.partial(_rwkv_kernel, pairs=pairs),
        grid=(B, W // bw, nc),
        in_specs=[seq] * 6 + [par] * 5,
        out_specs=seq,
        out_shape=jax.ShapeDtypeStruct((T, W), BF16),
        scratch_shapes=[pltpu.VMEM((pairs, LANES, LANES), F32)],
        compiler_params=_params("parallel", "parallel", "arbitrary"),
        name="rwkv_mix",
    )(r, k, v, ld, a, g, *(z.reshape(1, W) for z in (kk_scale, ka, rk, ln_w, ln_b)))


def _merge_kernel(ya_ref, yb_ref, yc_ref, pa_ref, pb_ref, pc_ref, ga_ref, gb_ref, gc_ref, o_ref):
    m = ga_ref[...].astype(F32) * _dot(ya_ref[...], pa_ref[...])
    m = m + gb_ref[...].astype(F32) * _dot(yb_ref[...], pb_ref[...])
    m = m + gc_ref[...].astype(F32) * _dot(yc_ref[...], pc_ref[...])
    o_ref[...] = m.astype(o_ref.dtype)


def branch_merge(ya, yb, yc, pa, pb, pc, gm, D):
    T = ya.shape[0]
    tm, tn = _pick(T, 1024), _pick(D, 512)
    nj = D // tn
    ysp = lambda y: pl.BlockSpec((tm, y.shape[1]), lambda i, j: (i, 0))
    psp = lambda p: pl.BlockSpec((p.shape[0], tn), lambda i, j: (0, j))
    gsp = lambda n: pl.BlockSpec((tm, tn), lambda i, j: (i, n * nj + j))
    return pl.pallas_call(
        _merge_kernel,
        grid=(T // tm, nj),
        in_specs=[ysp(ya), ysp(yb), ysp(yc), psp(pa), psp(pb), psp(pc), gsp(0), gsp(1), gsp(2)],
        out_specs=pl.BlockSpec((tm, tn), lambda i, j: (i, j)),
        out_shape=jax.ShapeDtypeStruct((T, D), BF16),
        compiler_params=_params("parallel", "arbitrary"),
        name="branch_merge",
    )(ya, yb, yc, pa, pb, pc, gm, gm, gm)


MOE_DMA_UNROLL = 32
SCATTER_DMA_PRIORITY = 1


def _moe_kernel(blk_e_ref, asg_ref, h_hbm, wg_ref, bg_ref, wu_ref, bu_ref, wd_ref, bd_ref, y_hbm,
                xbuf, ybuf, gsem, ssem, *, R, n_blk, A):
    i = pl.program_id(0)
    slot = i % 2
    D = xbuf.shape[-1]

    def per_row(fn, unroll=MOE_DMA_UNROLL):
        def body(r, c):
            fn(r)
            return c
        lax.fori_loop(0, R, body, 0, unroll=unroll)

    def start_gather(step, s, unroll=MOE_DMA_UNROLL):
        def one(r):
            tok = jnp.minimum(asg_ref[step * R + r] >> TOP_K_SHIFT, A // TOP_K - 1)
            pltpu.make_async_copy(h_hbm.at[pl.ds(tok, 1)], xbuf.at[s, pl.ds(r, 1)], gsem.at[s]).start()
        per_row(one, unroll)

    def start_scatter(step, s):
        def one(r):
            dst = asg_ref[step * R + r]
            col = pl.multiple_of((dst & (TOP_K - 1)) * D, LANES)
            pltpu.make_async_copy(ybuf.at[s, pl.ds(r, 1)],
                                  y_hbm.at[pl.ds(dst >> TOP_K_SHIFT, 1), pl.ds(col, D)],
                                  ssem.at[s]).start(priority=SCATTER_DMA_PRIORITY)
        per_row(one)

    def wait_gather(s):
        pltpu.make_async_copy(h_hbm.at[pl.ds(0, R)], xbuf.at[s], gsem.at[s]).wait()

    def wait_scatter(s):
        pltpu.make_async_copy(ybuf.at[s], y_hbm.at[pl.ds(0, R), pl.ds(0, D)], ssem.at[s]).wait()

    @pl.when(i == 0)
    def _():
        start_gather(0, 0)

    wait_gather(slot)
    x = xbuf[slot].astype(BF16)
    start_gather(jnp.minimum(i + 1, n_blk - 1), 1 - slot, unroll=True)
    gt = jnp.minimum(_dot(x, wg_ref[...]) + bg_ref[...], SWIGLU_LIMIT)
    up = jnp.clip(_dot(x, wu_ref[...]) + bu_ref[...], -SWIGLU_LIMIT, SWIGLU_LIMIT)
    act = gt * jax.nn.sigmoid(SWIGLU_ALPHA * gt) * (up + 1.0)
    y = _dot(act.astype(BF16), wd_ref[...]) + bd_ref[...]

    @pl.when(i >= 2)
    def _():
        wait_scatter(slot)

    ybuf[slot] = y
    start_scatter(i, slot)

    @pl.when(i == n_blk - 1)
    def _():
        wait_gather(1 - slot)
        wait_scatter(1 - slot)
        wait_scatter(slot)


def moe_experts(h, blk_e, row_asg, wg, bg, wu, bu, wd, bd, R):
    T, D = h.shape
    E, _, F = wg.shape
    n_blk = blk_e.shape[0]
    A = T * TOP_K
    assert n_blk >= 2 and (n_blk * R) % TOP_K == 0 and 1 << TOP_K_SHIFT == TOP_K
    wspec = lambda shape: pl.BlockSpec((None,) + shape, lambda i, be, ra: (be[i], 0, 0))
    return pl.pallas_call(
        functools.partial(_moe_kernel, R=R, n_blk=n_blk, A=A),
        grid_spec=pltpu.PrefetchScalarGridSpec(
            num_scalar_prefetch=2,
            grid=(n_blk,),
            in_specs=[pl.BlockSpec(memory_space=pl.ANY),
                      wspec((D, F)), wspec((1, F)), wspec((D, F)), wspec((1, F)),
                      wspec((F, D)), wspec((1, D))],
            out_specs=pl.BlockSpec(memory_space=pl.ANY),
            scratch_shapes=[pltpu.VMEM((2, R, D), F32), pltpu.VMEM((2, R, D), F32),
                            pltpu.SemaphoreType.DMA((2,)), pltpu.SemaphoreType.DMA((2,))]),
        out_shape=jax.ShapeDtypeStruct((n_blk * R // TOP_K, TOP_K * D), F32),
        compiler_params=_params("arbitrary"),
        name="moe_experts",
    )(blk_e, row_asg, h, wg, bg.reshape(E, 1, F), wu, bu.reshape(E, 1, F), wd, bd.reshape(E, 1, D))


def _combined(x_ref, y_ref, wt_ref, gate_ref):
    D = x_ref.shape[1]
    wt = wt_ref[...]
    acc = wt[:, 0:1] * y_ref[:, 0:D]
    for k in range(1, TOP_K):
        acc = acc + wt[:, k:k + 1] * y_ref[:, k * D:(k + 1) * D]
    return x_ref[...] + gate_ref[0] * acc


def _combine_next_kernel(x_ref, y_ref, wt_ref, gate_ref, g_ref, sc_ref, sh_ref, o_ref, h_ref):
    x = _combined(x_ref, y_ref, wt_ref, gate_ref)
    o_ref[...] = x
    h_ref[...] = _norm_mod(x, g_ref[...], sc_ref[0], sh_ref[0]).astype(h_ref.dtype)


def _combine_final_kernel(x_ref, y_ref, wt_ref, gate_ref, g_ref, o_ref):
    x = _combined(x_ref, y_ref, wt_ref, gate_ref)
    ms = jnp.mean(x * x, axis=-1, keepdims=True)
    o_ref[...] = x * lax.rsqrt(ms + RMS_EPS) * g_ref[...]


def moe_combine(x, y, wt, gate, S, norm_g, scale=None, shift=None):
    T, D = x.shape
    B = gate.shape[0]
    tm = _pick(S, 128)
    spt = S // tm
    row = pl.BlockSpec((tm, D), lambda i: (i, 0))
    per_batch = pl.BlockSpec((1, 1, D), lambda i: (i // spt, 0, 0))
    in_specs = [row, pl.BlockSpec((tm, TOP_K * D), lambda i: (i, 0)),
                pl.BlockSpec((tm, LANES), lambda i: (i, 0)), per_batch,
                pl.BlockSpec((1, D), lambda i: (0, 0))]
    args = [x, y, wt, gate.reshape(B, 1, D), norm_g.reshape(1, D)]
    if scale is None:
        return pl.pallas_call(
            _combine_final_kernel, grid=(T // tm,), in_specs=in_specs, out_specs=row,
            out_shape=jax.ShapeDtypeStruct((T, D), F32),
            compiler_params=_params("parallel"), name="moe_combine_final")(*args)
    return pl.pallas_call(
        _combine_next_kernel, grid=(T // tm,), in_specs=in_specs + [per_batch, per_batch],
        out_specs=[row, row],
        out_shape=[jax.ShapeDtypeStruct((T, D), F32), jax.ShapeDtypeStruct((T, D), BF16)],
        compiler_params=_params("parallel"), name="moe_combine_next",
    )(*args, scale.reshape(B, 1, D), shift.reshape(B, 1, D))


def route_tables(top_idx, rank, counts, E, R):
    T = top_idx.shape[0]
    A = T * TOP_K
    flat_e = top_idx.reshape(A)
    padded = (counts + R - 1) // R * R
    pends = jnp.cumsum(padded)
    onehot = flat_e[:, None] == jnp.arange(E, dtype=I32)[None, :]
    dest = jnp.sum(jnp.where(onehot, (pends - padded)[None, :], 0), axis=1) + rank.reshape(A)
    n_blk = -(-(A + E * (R - 1)) // R)
    blk_start = jnp.arange(n_blk, dtype=I32)[:, None] * R
    blk_e = jnp.minimum(jnp.sum((pends[None, :] <= blk_start).astype(I32), axis=1), E - 1)
    pads_before = jnp.cumsum(padded - counts) - (padded - counts)
    r = blk_start + jnp.arange(R, dtype=I32)[None, :] - (pends - padded)[blk_e][:, None]
    over = r - counts[blk_e][:, None]
    pad_id = (A + pads_before[blk_e][:, None] + over).reshape(n_blk * R)
    real = jnp.full((n_blk * R,), -1, I32).at[dest].set(jnp.arange(A, dtype=I32))
    row_asg = jnp.where(real < 0, pad_id, real)
    return blk_e, row_asg


def _mixer_weights(w_in, D):
    CW, QW, GW = CONV_WIDTH, NSA_Q_WIDTH, NSA_G_WIDTH
    o_q = 3 * CW
    o_kv = o_q + QW
    o_g = o_kv + 6 * GW
    o_r = o_g + NSA_GATE_WIDTH
    o_m = o_r + RWKV_IN_WIDTH
    w_qkv = w_in[:, :o_g].astype(BF16)
    per_group = NSA_GATE_WIDTH // NSA_KV_GROUPS
    gate_blocks = [jnp.pad(w_in[:, o_g + g * per_group:o_g + (g + 1) * per_group],
                           ((0, 0), (0, LANES - per_group))) for g in range(NSA_KV_GROUPS)]
    w_rg = jnp.concatenate(
        [jnp.pad(w_in[:, o_r:o_m], ((0, 0), (0, RWKV_PAD_WIDTH - RWKV_IN_WIDTH)))] + gate_blocks,
        axis=1).astype(BF16)
    w_m = w_in[:, o_m:].astype(BF16)
    return w_qkv, w_rg, w_m


def _layer(x, h, mod, next_norm, final_g, B, S, tables, norm_g, w_in, conv_w, cmp_pe, cmp_w1, cmp_w2, rwkv_mu, rwkv_w0,
           rwkv_wb, rwkv_a0, rwkv_ab, rwkv_gb, rwkv_kk, rwkv_ka, rwkv_rk, rwkv_ln_w, rwkv_ln_b,
           w_branch, w_out, router_w, router_b, exp_wg, exp_bg, exp_wu, exp_bu, exp_wd, exp_bd):
    T, D = x.shape
    CW, QW, GW, Dh, G = CONV_WIDTH, NSA_Q_WIDTH, NSA_G_WIDTH, NSA_HEAD_DIM, NSA_KV_GROUPS
    shift1, scale1, gate1, shift2, scale2, gate2 = (mod[:, n] for n in range(N_MOD))

    w_qkv, w_rg, w_m = _mixer_weights(w_in, D)
    z = mm_qkv(h, w_qkv, tables, S)
    zrg = _mm_call(_mm_plain_kernel, h, w_rg, F32, name="mm_rwkv_gate")
    gm = _mm_call(_mm_sigmoid_kernel, h, w_m, BF16, name="mm_merge_gate")

    y_a = gated_conv(z, conv_w, S)

    ns = S // CMP_STRIDE
    kv0 = 3 * CW + QW
    to_strides = lambda zz: zz.reshape(B, ns, CMP_STRIDE, G, Dh).transpose(0, 3, 1, 2, 4).reshape(
        B, G, ns, CMP_STRIDE * Dh)
    cmp_in = jnp.stack([to_strides(z[:, kv0:kv0 + GW]), to_strides(z[:, kv0 + GW:kv0 + 2 * GW])])
    kv_cmp = compress_kv(cmp_in, cmp_pe, cmp_w1, cmp_w2)
    y_b = nsa_attention(z, zrg, kv_cmp, B, S, q_col=3 * CW, ks_col=kv0 + 2 * GW, vs_col=kv0 + 3 * GW,
                        kw_col=kv0 + 4 * GW, vw_col=kv0 + 5 * GW, gate_col=RWKV_PAD_WIDTH)

    r, k, v, ld, a, g = rwkv_prep(zrg, rwkv_mu, rwkv_w0, rwkv_wb, rwkv_a0, rwkv_ab, rwkv_gb, S)
    y_c = rwkv_mix(r, k, v, ld, a, g, rwkv_kk, rwkv_ka, rwkv_rk.reshape(-1), rwkv_ln_w, rwkv_ln_b, B, S)

    wb16 = w_branch.astype(BF16)
    m = branch_merge(y_a, y_b, y_c, wb16[:CW], wb16[CW:CW + QW], wb16[CW + QW:], gm, D)
    x = mm_residual(m, w_out.astype(BF16), x, gate1, S)

    E = router_w.shape[1]
    h2, top_idx, top_w, counts = norm_router(x, norm_g[1], scale2, shift2, router_w, router_b, S)
    R = _pick(T * TOP_K // E, 256)
    blk_e, row_asg = route_tables(top_idx[:, :TOP_K], top_idx[:, TOP_K:2 * TOP_K], counts[0, :E], E, R)
    y = moe_experts(h2, blk_e, row_asg, exp_wg.astype(BF16), exp_bg, exp_wu.astype(BF16), exp_bu,
                    exp_wd.astype(BF16), exp_bd, R)
    if next_norm is None:
        return moe_combine(x, y, top_w, gate2, S, final_g)
    return moe_combine(x, y, top_w, gate2, S, *next_norm)


def kernel(x, c, ada_w, ada_b, ada_table, norm_g, final_g, w_in, conv_w, cmp_pe, cmp_w1, cmp_w2, rwkv_mu, rwkv_w0, rwkv_wb, rwkv_a0, rwkv_ab, rwkv_gb, rwkv_kk, rwkv_ka, rwkv_rk, rwkv_ln_w, rwkv_ln_b, w_branch, w_out, router_w, router_b, exp_wg, exp_bg, exp_wu, exp_bu, exp_wd, exp_bd):
    B, S, D = x.shape
    depth = w_in.shape[0]
    tables = rope_lane_tables(S)
    mod_all = ada_modulation(c, ada_w, ada_b).reshape(B, N_MOD, D)
    xf = x.reshape(B * S, D)
    mods = [mod_all + ada_table[l] for l in range(depth)]
    h = norm_modulate(xf, norm_g[0, 0], mods[0][:, 1], mods[0][:, 0], S, BF16)
    for l in range(depth):
        next_norm = None if l + 1 == depth else (norm_g[l + 1, 0], mods[l + 1][:, 1], mods[l + 1][:, 0])
        out = _layer(xf, h, mods[l], next_norm, final_g, B, S, tables, norm_g[l], w_in[l], conv_w[l],
                     cmp_pe[l], cmp_w1[l], cmp_w2[l], rwkv_mu[l], rwkv_w0[l], rwkv_wb[l], rwkv_a0[l],
                     rwkv_ab[l], rwkv_gb[l], rwkv_kk[l], rwkv_ka[l], rwkv_rk[l], rwkv_ln_w[l],
                     rwkv_ln_b[l], w_branch[l], w_out[l], router_w[l], router_b[l], exp_wg[l], exp_bg[l],
                     exp_wu[l], exp_bu[l], exp_wd[l], exp_bd[l])
        if next_norm is None:
            return out.reshape(B, S, D)
        xf, h = out
```

```python
import functools

import numpy as np
import jax
import jax.numpy as jnp
from jax import lax
from jax.experimental import pallas as pl
from jax.experimental.pallas import tpu as pltpu

F32 = jnp.float32
BF16 = jnp.bfloat16
I32 = jnp.int32
HI = lax.Precision.HIGHEST

N_MOD = 6
RMS_EPS = 1e-6
CONV_WIDTH = 1024
CONV_K = 3
NSA_HEADS = 16
NSA_KV_GROUPS = 4
NSA_HPG = NSA_HEADS // NSA_KV_GROUPS
NSA_HEAD_DIM = 128
CMP_BLOCK = 32
CMP_STRIDE = 16
SEL_BLOCK = 64
N_SELECT = 16
WINDOW = 512
FORCE_SCORE = 1e9
NEG_INF = -1e30
ROPE_THETA = 500000.0
ROPE_DIMS = NSA_HEAD_DIM // 4
NSA_Q_WIDTH = NSA_HEADS * NSA_HEAD_DIM
NSA_G_WIDTH = NSA_KV_GROUPS * NSA_HEAD_DIM
NSA_GATE_WIDTH = 3 * NSA_HEADS
RWKV_HEADS = 16
RWKV_HEAD_DIM = 64
RWKV_WIDTH = RWKV_HEADS * RWKV_HEAD_DIM
RWKV_DECAY_LORA = 64
RWKV_A_LORA = 64
RWKV_GATE_LORA = 160
RWKV_IN_WIDTH = 3 * RWKV_WIDTH + RWKV_DECAY_LORA + RWKV_A_LORA + RWKV_GATE_LORA
RWKV_LN_EPS = 64e-5
RWKV_CHUNK = 64
TOP_K = 4
TOP_K_SHIFT = 2
SWIGLU_ALPHA = 1.702
SWIGLU_LIMIT = 7.0

LANES = 128
SUBLANES = 8
VMEM_BYTES_V7X = 64 * 1024 * 1024
VMEM_LIMIT = VMEM_BYTES_V7X - 12 * 1024 * 1024

RWKV_PAD_WIDTH = -(-RWKV_IN_WIDTH // 512) * 512
RWKV_TAIL = 3 * RWKV_WIDTH


def _params(*sem):
    return pltpu.CompilerParams(dimension_semantics=sem, vmem_limit_bytes=VMEM_LIMIT)


def _pick(n, pref):
    t = min(n, pref)
    while n % t:
        t //= 2
    return t


def _dot(a, b, **kw):
    return jnp.dot(a, b, preferred_element_type=F32, **kw)


def _dot_nt(a, b, **kw):
    return lax.dot_general(a, b, (((1,), (1,)), ((), ())), preferred_element_type=F32, **kw)


def _dot_tn(a, b, **kw):
    return lax.dot_general(a, b, (((0,), (0,)), ((), ())), preferred_element_type=F32, **kw)


def _ada_kernel(c_ref, w_ref, b_ref, o_ref):
    c = c_ref[...]
    sc = c * jax.nn.sigmoid(c)
    o_ref[...] = _dot(sc, w_ref[...], precision=HI) + b_ref[...]


def ada_modulation(c, ada_w, ada_b):
    B, D = c.shape
    N = ada_w.shape[1]
    rows = -(-B // SUBLANES) * SUBLANES
    cp = jnp.zeros((rows, D), F32).at[:B].set(c)
    tn = _pick(N, 512)
    out = pl.pallas_call(
        _ada_kernel,
        grid=(N // tn,),
        in_specs=[pl.BlockSpec((rows, D), lambda j: (0, 0)),
                  pl.BlockSpec((D, tn), lambda j: (0, j)),
                  pl.BlockSpec((1, tn), lambda j: (0, j))],
        out_specs=pl.BlockSpec((rows, tn), lambda j: (0, j)),
        out_shape=jax.ShapeDtypeStruct((rows, N), F32),
        compiler_params=_params("parallel"),
        name="ada_mod",
    )(cp, ada_w, ada_b.reshape(1, N))
    return out[:B]


def _norm_mod(x, g, sc, sh):
    ms = jnp.mean(x * x, axis=-1, keepdims=True)
    y = x * lax.rsqrt(ms + RMS_EPS) * g
    return y * (1.0 + sc) + sh


def _pack_bf16_pairs(v):
    half = v.shape[1] // 2
    bits = lax.bitcast_convert_type(v.astype(BF16).astype(F32), jnp.uint32)
    return (bits[:, half:] & jnp.uint32(0xFFFF0000)) | (bits[:, :half] >> 16)


def _unpack_bf16_pairs(w):
    lo = lax.bitcast_convert_type(w << 16, F32)
    hi = lax.bitcast_convert_type(w & jnp.uint32(0xFFFF0000), F32)
    return lo, hi


def _norm_mod_kernel(x_ref, g_ref, sc_ref, sh_ref, o_ref):
    h = _norm_mod(x_ref[...], g_ref[...], sc_ref[0], sh_ref[0])
    o_ref[...] = h.astype(o_ref.dtype)


def norm_modulate(x, g, scale, shift, S, out_dtype):
    T, D = x.shape
    B = scale.shape[0]
    tm = _pick(S, 256)
    spt = S // tm
    return pl.pallas_call(
        _norm_mod_kernel,
        grid=(T // tm,),
        in_specs=[pl.BlockSpec((tm, D), lambda i: (i, 0)),
                  pl.BlockSpec((1, D), lambda i: (0, 0)),
                  pl.BlockSpec((1, 1, D), lambda i: (i // spt, 0, 0)),
                  pl.BlockSpec((1, 1, D), lambda i: (i // spt, 0, 0))],
        out_specs=pl.BlockSpec((tm, D), lambda i: (i, 0)),
        out_shape=jax.ShapeDtypeStruct((T, D), out_dtype),
        compiler_params=_params("parallel"),
        name="norm_mod",
    )(x, g.reshape(1, D), scale.reshape(B, 1, D), shift.reshape(B, 1, D))


def _norm_router_kernel(x_ref, g_ref, sc_ref, sh_ref, rw_ref, rb_ref, h_ref, idx_ref, wt_ref, cnt_ref,
                        base_sc):
    @pl.when(pl.program_id(0) == 0)
    def _():
        base_sc[...] = jnp.zeros(base_sc.shape, F32)

    h = _norm_mod(x_ref[...], g_ref[...], sc_ref[0], sh_ref[0])
    h_ref[...] = _pack_bf16_pairs(h)
    logits = _dot(h, rw_ref[...], precision=HI) + rb_ref[...]
    tm = logits.shape[0]
    lane = lax.broadcasted_iota(I32, logits.shape, 1)
    lanef = lane.astype(F32)
    idx_out = jnp.zeros(logits.shape, I32)
    val_out = jnp.zeros(logits.shape, F32)
    v0 = None
    hits = []
    for k in range(TOP_K):
        mx = jnp.max(logits, axis=1, keepdims=True)
        first = jnp.min(jnp.where(logits == mx, lanef, float(LANES)), axis=1, keepdims=True)
        hit = lanef == first
        hits.append(jnp.where(hit, 1.0, 0.0))
        if k == 0:
            v0 = mx
        idx_out = jnp.where(lane == k, first.astype(I32), idx_out)
        val_out = jnp.where(lane == k, jnp.exp(mx - v0), val_out)
        logits = jnp.where(hit, -3e38, logits)
    wt_ref[...] = val_out / jnp.sum(val_out, axis=1, keepdims=True)
    used = hits[0]
    for k in range(1, TOP_K):
        used = used + hits[k]
    earlier = jnp.where(lax.broadcasted_iota(I32, (tm, tm), 1) < lax.broadcasted_iota(I32, (tm, tm), 0),
                        1.0, 0.0).astype(BF16)
    before = base_sc[0:1] + _dot(earlier, used.astype(BF16))
    for k in range(TOP_K):
        rank = jnp.sum(hits[k] * before, axis=1, keepdims=True)
        idx_out = jnp.where(lane == TOP_K + k, rank.astype(I32), idx_out)
    idx_ref[...] = idx_out
    total = base_sc[0:1] + jnp.sum(used, axis=0, keepdims=True)
    base_sc[...] = jnp.broadcast_to(total, base_sc.shape)
    cnt_ref[...] = jnp.broadcast_to(total, cnt_ref.shape).astype(I32)


def norm_router(x, g, scale, shift, router_w, router_b, S):
    T, D = x.shape
    B = scale.shape[0]
    E = router_w.shape[1]
    assert E <= LANES
    rw = jnp.zeros((D, LANES), F32).at[:, :E].set(router_w)
    rb = jnp.full((1, LANES), NEG_INF, F32).at[0, :E].set(router_b)
    tm = _pick(S, 256)
    spt = S // tm
    return pl.pallas_call(
        _norm_router_kernel,
        grid=(T // tm,),
        in_specs=[pl.BlockSpec((tm, D), lambda i: (i, 0)),
                  pl.BlockSpec((1, D), lambda i: (0, 0)),
                  pl.BlockSpec((1, 1, D), lambda i: (i // spt, 0, 0)),
                  pl.BlockSpec((1, 1, D), lambda i: (i // spt, 0, 0)),
                  pl.BlockSpec((D, LANES), lambda i: (0, 0)),
                  pl.BlockSpec((1, LANES), lambda i: (0, 0))],
        out_specs=[pl.BlockSpec((tm, D // 2), lambda i: (i, 0)),
                   pl.BlockSpec((tm, LANES), lambda i: (i, 0)),
                   pl.BlockSpec((tm, LANES), lambda i: (i, 0)),
                   pl.BlockSpec((SUBLANES, LANES), lambda i: (0, 0))],
        out_shape=[jax.ShapeDtypeStruct((T, D // 2), jnp.uint32),
                   jax.ShapeDtypeStruct((T, LANES), I32),
                   jax.ShapeDtypeStruct((T, LANES), F32),
                   jax.ShapeDtypeStruct((SUBLANES, LANES), I32)],
        scratch_shapes=[pltpu.VMEM((SUBLANES, LANES), F32)],
        compiler_params=_params("arbitrary"),
        name="norm_router",
    )(x, g.reshape(1, D), scale.reshape(B, 1, D), shift.reshape(B, 1, D), rw, rb)


def _rope_lanes(x, cf, sa, sb):
    half = ROPE_DIMS // 2
    return x * cf + pltpu.roll(x, LANES - half, 1) * sa + pltpu.roll(x, half, 1) * sb


def _mm_plain_kernel(a_ref, b_ref, o_ref):
    o_ref[...] = _dot(a_ref[...], b_ref[...]).astype(o_ref.dtype)


def _mm_sigmoid_kernel(a_ref, b_ref, o_ref):
    o_ref[...] = jax.nn.sigmoid(_dot(a_ref[...], b_ref[...])).astype(o_ref.dtype)


def _mm_qkv_kernel(a_ref, b_ref, cf_ref, sa_ref, sb_ref, o_ref, *, q_lo, q_hi, kv_lo):
    j = pl.program_id(1)
    is_q = (j >= q_lo) & (j < q_hi)
    is_key = (j >= kv_lo) & ((j - kv_lo) % 2 == 0)
    acc = _dot(a_ref[...], b_ref[...])

    @pl.when(is_q | is_key)
    def _():
        fac = jnp.where(is_q, NSA_HEAD_DIM ** -0.5, 1.0)
        cf, sa, sb = cf_ref[...], sa_ref[...], sb_ref[...]
        for n in range(acc.shape[1] // LANES):
            sl = slice(n * LANES, (n + 1) * LANES)
            o_ref[:, sl] = _rope_lanes(acc[:, sl] * fac, cf, sa, sb).astype(o_ref.dtype)

    @pl.when(jnp.logical_not(is_q | is_key))
    def _():
        o_ref[...] = acc.astype(o_ref.dtype)


def _mm_residual_kernel(a_ref, b_ref, x_ref, gate_ref, o_ref):
    o_ref[...] = x_ref[...] + gate_ref[0] * _dot(a_ref[...], b_ref[...])


def _mm_call(kernel, a, b, out_dtype, extra=(), extra_specs=(), tm_pref=1024, tn_pref=512, name="mm"):
    M, K = a.shape
    N = b.shape[1]
    tm, tn = _pick(M, tm_pref), _pick(N, tn_pref)
    return pl.pallas_call(
        kernel,
        grid=(M // tm, N // tn),
        in_specs=[pl.BlockSpec((tm, K), lambda i, j: (i, 0)),
                  pl.BlockSpec((K, tn), lambda i, j: (0, j))] + [s(tm, tn) for s in extra_specs],
        out_specs=pl.BlockSpec((tm, tn), lambda i, j: (i, j)),
        out_shape=jax.ShapeDtypeStruct((M, N), out_dtype),
        compiler_params=_params("parallel", "arbitrary"),
        name=name,
    )(a, b, *extra)


def rope_lane_tables(S):
    half = ROPE_DIMS // 2
    inv = ROPE_THETA ** (-jnp.arange(0, ROPE_DIMS, 2, dtype=F32) / ROPE_DIMS)
    ang = jnp.arange(S, dtype=F32)[:, None] * inv[None, :]
    cos, sin = jnp.cos(ang), jnp.sin(ang)
    zeros = jnp.zeros((S, LANES - ROPE_DIMS), F32)
    zh = jnp.zeros((S, half), F32)
    cf = jnp.concatenate([cos, cos, zeros + 1.0], axis=1)
    sa = jnp.concatenate([-sin, zh, zeros], axis=1)
    sb = jnp.concatenate([zh, sin, zeros], axis=1)
    return cf, sa, sb


def mm_qkv(a, b, tables, S):
    T = a.shape[0]
    tm = _pick(min(T, S), 1024)
    spt = S // tm
    spec = lambda tm_, tn_: pl.BlockSpec((tm_, LANES), lambda i, j: (i % spt, 0))
    tn = NSA_G_WIDTH
    q0, kv0 = 3 * CONV_WIDTH, 3 * CONV_WIDTH + NSA_Q_WIDTH
    assert q0 % tn == 0 and kv0 % tn == 0 and b.shape[1] == kv0 + 6 * tn
    kern = functools.partial(_mm_qkv_kernel, q_lo=q0 // tn, q_hi=kv0 // tn, kv_lo=kv0 // tn)
    return _mm_call(kern, a, b, BF16, extra=tables, extra_specs=(spec,) * 3, tm_pref=tm, tn_pref=tn,
                    name="mm_qkv")


def mm_residual(a, b, x, gate, S):
    T = a.shape[0]
    B, D = gate.shape
    tm = _pick(min(T, S), 1024)
    spt = S // tm
    xspec = lambda tm_, tn_: pl.BlockSpec((tm_, tn_), lambda i, j: (i, j))
    gspec = lambda tm_, tn_: pl.BlockSpec((1, 1, tn_), lambda i, j: (i // spt, 0, j))
    return _mm_call(_mm_residual_kernel, a, b, F32, extra=(x, gate.reshape(B, 1, D)),
                    extra_specs=(xspec, gspec), tm_pref=tm, name="mm_residual")


def _conv_kernel(z_ref, zp_ref, w_ref, o_ref, *, tm, S):
    i = pl.program_id(0)
    W = CONV_WIDTH
    z = z_ref[...].astype(F32)
    cb, u = z[:, :W], z[:, W:2 * W] * z[:, 2 * W:]
    zp = zp_ref[...].astype(F32)
    up = zp[:, W:2 * W] * zp[:, 2 * W:]
    up = jnp.where((i * tm) % S == 0, 0.0, up)
    row = lax.broadcasted_iota(I32, u.shape, 0)
    last, prev = up[SUBLANES - 1:SUBLANES], up[SUBLANES - 2:SUBLANES - 1]
    u1 = jnp.where(row == 0, last, pltpu.roll(u, 1, 0))
    u2 = jnp.where(row == 0, prev, jnp.where(row == 1, last, pltpu.roll(u, 2, 0)))
    w = w_ref[...]
    o_ref[...] = (cb * (w[0:1] * u2 + w[1:2] * u1 + w[2:3] * u)).astype(o_ref.dtype)


def gated_conv(z, conv_w, S):
    T = z.shape[0]
    W = CONV_WIDTH
    assert CONV_K == 3
    tm = _pick(S, 512)
    wp = jnp.zeros((SUBLANES, W), F32).at[:CONV_K].set(conv_w)
    return pl.pallas_call(
        functools.partial(_conv_kernel, tm=tm, S=S),
        grid=(T // tm,),
        in_specs=[pl.BlockSpec((tm, 3 * W), lambda i: (i, 0)),
                  pl.BlockSpec((SUBLANES, 3 * W), lambda i: (jnp.maximum(i * (tm // SUBLANES) - 1, 0), 0)),
                  pl.BlockSpec((SUBLANES, W), lambda i: (0, 0))],
        out_specs=pl.BlockSpec((tm, W), lambda i: (i, 0)),
        out_shape=jax.ShapeDtypeStruct((T, W), BF16),
        compiler_params=_params("parallel"),
        name="gated_conv",
    )(z, z, wp)


def _compress_kernel(x_ref, w1_ref, pe_ref, w2_ref, o_ref):
    x = x_ref[...]
    w1 = w1_ref[...]
    half = x.shape[1]
    ns = x.shape[0]
    first = _dot(x, w1[:half].astype(BF16))
    second = _dot(x, w1[half:].astype(BF16))
    bias = _dot(pe_ref[...], w1, precision=HI)[0:1]
    hid = first + pltpu.roll(second, ns - 1, 0) + bias
    act = jax.nn.gelu(hid, approximate=True)
    o_ref[...] = _dot(act.astype(BF16), w2_ref[...].astype(BF16)).astype(o_ref.dtype)


def compress_kv(x, cmp_pe, cmp_w1, cmp_w2):
    _, B, G, NS, W = x.shape
    assert CMP_BLOCK == 2 * CMP_STRIDE
    Dh = NSA_HEAD_DIM
    hid = cmp_w1.shape[-1]
    pe = jnp.broadcast_to(cmp_pe.reshape(2, 1, CMP_BLOCK * Dh), (2, SUBLANES, CMP_BLOCK * Dh))
    return pl.pallas_call(
        _compress_kernel,
        grid=(2, B, G),
        in_specs=[pl.BlockSpec((None, None, None, NS, W), lambda s, b, g: (s, b, g, 0, 0)),
                  pl.BlockSpec((None, CMP_BLOCK * Dh, hid), lambda s, b, g: (s, 0, 0)),
                  pl.BlockSpec((None, SUBLANES, CMP_BLOCK * Dh), lambda s, b, g: (s, 0, 0)),
                  pl.BlockSpec((None, hid, Dh), lambda s, b, g: (s, 0, 0))],
        out_specs=pl.BlockSpec((None, None, None, NS, Dh), lambda s, b, g: (s, b, g, 0, 0)),
        out_shape=jax.ShapeDtypeStruct((2, B, G, NS, Dh), BF16),
        compiler_params=_params("parallel", "parallel", "parallel"),
        name="nsa_compress",
    )(x, cmp_w1, pe, cmp_w2)


def _nsa_kernel(q_ref, kc_ref, vc_ref, ks_ref, vs_ref, kw_ref, vw_ref, zg_ref, bmapt_ref, o_ref,
                s_sc, m_sc, acc_sc, out_sc, *, tq, tk, nsel):
    t0 = pl.program_id(2) * tq
    Dh = NSA_HEAD_DIM
    hpg = NSA_HPG
    nch = tk // LANES
    heads = range(hpg)
    trow = t0 + lax.broadcasted_iota(I32, (tq, 1), 0)
    gates = jax.nn.sigmoid(zg_ref[...])

    def q(h):
        return q_ref[:, h * Dh:(h + 1) * Dh]

    def gate(h, branch):
        return gates[:, 3 * h + branch:3 * h + branch + 1]

    kc, vc = kc_ref[...], vc_ref[...]
    ncb = kc.shape[0]
    blk_end = lax.broadcasted_iota(I32, (1, ncb), 1) * CMP_STRIDE + (CMP_BLOCK - 1)
    valid = blk_end <= trow
    bias_c = jnp.where(valid, 0.0, NEG_INF)
    validf = jnp.where(valid, 1.0, 0.0)
    psum = None
    for h in heads:
        sm = _dot_nt(q(h), kc) + bias_c
        p = jnp.exp(sm - jnp.max(sm, axis=1, keepdims=True)) * validf
        pc = p / jnp.maximum(jnp.sum(p, axis=1, keepdims=True), 1e-30)
        out_sc[h] = gate(h, 0) * _dot(pc.astype(BF16), vc)
        psum = pc if psum is None else psum + pc

    bmt = bmapt_ref[...]
    hi = psum.astype(BF16)
    rest = psum - hi.astype(F32)
    mid = rest.astype(BF16)
    lo = (rest - mid.astype(F32)).astype(BF16)
    imp = _dot_nt(bmt, hi) + _dot_nt(bmt, mid) + _dot_nt(bmt, lo)
    nb = imp.shape[0]
    jb = lax.broadcasted_iota(I32, (nb, tq), 0)
    cur = (t0 + lax.broadcasted_iota(I32, (1, tq), 1)) // SEL_BLOCK
    forced = (jb == 0) | (jb == cur) | (jb == cur - 1)
    imp = jnp.where(forced, FORCE_SCORE, imp)
    imp = jnp.where(jb <= cur, imp, NEG_INF)
    rank = jnp.zeros((nb, tq), F32)
    for i in range(nb):
        ri = imp[i:i + 1, :]
        rank = rank + jnp.where((ri > imp) | ((ri == imp) & (jb > i)), 1.0, 0.0)
    sel_t = jnp.where(rank < nsel, 1.0, 0.0)

    bpt = tk // SEL_BLOCK
    nkt = (t0 + tq + tk - 1) // tk
    m_sc[...] = jnp.full(m_sc.shape, NEG_INF, F32)
    acc_sc[...] = jnp.zeros(acc_sc.shape, F32)

    def chunks(a):
        return [a[:, c * LANES:(c + 1) * LANES] for c in range(nch)]

    def score_sweep(kt, carry):
        k = ks_ref[pl.ds(pl.multiple_of(kt * tk, tk), tk), :]
        expand = jnp.where(lax.broadcasted_iota(I32, (nb, tk), 0)
                           == kt * bpt + lax.broadcasted_iota(I32, (nb, tk), 1) // SEL_BLOCK, 1.0, 0.0)
        chosen = _dot_tn(sel_t, expand)
        kpos = kt * tk + lax.broadcasted_iota(I32, (1, tk), 1)
        bias = jnp.where((chosen > 0.5) & (kpos <= trow), 0.0, NEG_INF)
        for h in heads:
            s = _dot_nt(q(h), k) + bias
            s_sc[kt, h] = s
            m = m_sc[h]
            for sc in chunks(s):
                m = jnp.maximum(m, sc)
            m_sc[h] = m
        return carry

    lax.fori_loop(0, nkt, score_sweep, 0)
    for h in heads:
        m_sc[h] = jnp.broadcast_to(jnp.max(m_sc[h], axis=1, keepdims=True), (tq, LANES))
    ones = jnp.ones((tk, LANES), BF16)

    def exp_sweep(kt, carry):
        v1 = jnp.concatenate([vs_ref[pl.ds(pl.multiple_of(kt * tk, tk), tk), :], ones], axis=1)
        for h in heads:
            m = m_sc[h]
            p = jnp.concatenate([jnp.exp(sc - m) for sc in chunks(s_sc[kt, h])], axis=1)
            acc_sc[h] = acc_sc[h] + _dot(p.astype(BF16), v1)
        return carry

    lax.fori_loop(0, nkt, exp_sweep, 0)
    for h in heads:
        acc = acc_sc[h]
        out_sc[h] = out_sc[h] + gate(h, 1) * (acc[:, :Dh] / acc[:, Dh:Dh + 1])

    span = WINDOW + tq
    w0 = pl.multiple_of(jnp.maximum(t0 - WINDOW, 0), tq)
    kwv = kw_ref[pl.ds(w0, span), :]
    vwv = vw_ref[pl.ds(w0, span), :]
    diff = trow - (w0 + lax.broadcasted_iota(I32, (1, span), 1))
    bias_w = jnp.where((diff >= 0) & (diff < WINDOW), 0.0, NEG_INF)
    for h in heads:
        sw = _dot_nt(q(h), kwv) + bias_w
        pw = jnp.exp(sw - jnp.max(sw, axis=1, keepdims=True))
        o_w = _dot(pw.astype(BF16), vwv) / jnp.sum(pw, axis=1, keepdims=True)
        o_ref[:, h * Dh:(h + 1) * Dh] = (out_sc[h] + gate(h, 2) * o_w).astype(o_ref.dtype)


def block_map(nc, nb):
    i = np.arange(nc)[:, None] * CMP_STRIDE
    j = np.arange(nb)[None, :] * SEL_BLOCK
    inter = np.clip(np.minimum(i + CMP_BLOCK, j + SEL_BLOCK) - np.maximum(i, j), 0, None)
    return jnp.asarray(inter.astype(np.float32) / np.float32(CMP_BLOCK))


def nsa_attention(z, zgate, kv_cmp, B, S, q_col, ks_col, vs_col, kw_col, vw_col, gate_col):
    T = B * S
    G, Dh, hpg = NSA_KV_GROUPS, NSA_HEAD_DIM, NSA_HPG
    GW = G * Dh
    tq = 256
    tk = _pick(S, 512)
    assert S % tq == 0 and WINDOW % tq == 0 and S >= WINDOW + tq and tk % SEL_BLOCK == 0
    NS = kv_cmp.shape[3]
    nb = S // SEL_BLOCK
    nq = S // tq
    bmap_t = block_map(NS, nb).T.astype(BF16)
    assert CMP_BLOCK in (32, 64, 128)
    assert q_col % (hpg * Dh) == 0 and all(c % Dh == 0 for c in (ks_col, vs_col, kw_col, vw_col))
    qb0 = q_col // (hpg * Dh)
    ks_b, vs_b, kw_b, vw_b = (c // Dh for c in (ks_col, vs_col, kw_col, vw_col))
    g_b = gate_col // LANES
    head_f32 = pltpu.VMEM((hpg, tq, Dh), F32)
    cmp_spec = lambda s: pl.BlockSpec((None, None, None, NS, Dh), lambda b, g, i: (s, b, g, 0, 0))
    seq_spec = lambda blk: pl.BlockSpec((S, Dh), lambda b, g, i: (b, blk + g))
    return pl.pallas_call(
        functools.partial(_nsa_kernel, tq=tq, tk=tk, nsel=min(N_SELECT, nb)),
        grid=(B, G, nq),
        in_specs=[pl.BlockSpec((tq, hpg * Dh), lambda b, g, i: (b * nq + i, qb0 + g)),
                  cmp_spec(0), cmp_spec(1),
                  seq_spec(ks_b), seq_spec(vs_b), seq_spec(kw_b), seq_spec(vw_b),
                  pl.BlockSpec((tq, LANES), lambda b, g, i: (b * nq + i, g_b + g)),
                  pl.BlockSpec((nb, NS), lambda b, g, i: (0, 0))],
        out_specs=pl.BlockSpec((tq, hpg * Dh), lambda b, g, i: (b * nq + i, g)),
        out_shape=jax.ShapeDtypeStruct((T, NSA_Q_WIDTH), BF16),
        scratch_shapes=[pltpu.VMEM((S // tk, hpg, tq, tk), F32), head_f32,
                        pltpu.VMEM((hpg, tq, 2 * Dh), F32), head_f32],
        compiler_params=_params("parallel", "parallel", "arbitrary"),
        name="nsa_attention",
    )(z, kv_cmp, kv_cmp, z, z, z, z, zgate, bmap_t)


def _rwkv_prep_kernel(u_ref, up_ref, mu_ref, w0_ref, a0_ref, wb_ref, ab_ref, gb_ref,
                      r_ref, k_ref, v_ref, ld_ref, a_ref, g_ref, *, tm, S):
    i = pl.program_id(0)
    W = RWKV_WIDTH
    u = u_ref[...]
    prev = jnp.where((i * tm) % S == 0, 0.0, up_ref[...][SUBLANES - 1:SUBLANES])
    row = lax.broadcasted_iota(I32, u.shape, 0)
    u_prev = jnp.where(row == 0, prev, pltpu.roll(u, 1, 0))
    um = u + (u_prev - u) * mu_ref[...]
    r_ref[...] = um[:, :W]
    k_ref[...] = um[:, W:2 * W]
    v_ref[...] = um[:, 2 * W:3 * W]
    lora = um[:, RWKV_TAIL:RWKV_TAIL + LANES]
    gl = um[:, RWKV_TAIL + LANES:RWKV_TAIL + LANES + gb_ref.shape[0]]
    y = w0_ref[...] + _dot(jnp.tanh(lora), wb_ref[...], precision=HI)
    ld_ref[...] = -float(np.exp(-0.5)) * jax.nn.sigmoid(y)
    a_ref[...] = jax.nn.sigmoid(a0_ref[...] + _dot(lora, ab_ref[...], precision=HI))
    g_ref[...] = _dot(jax.nn.sigmoid(gl), gb_ref[...], precision=HI)


def rwkv_prep(zr, rwkv_mu, w0, wb, a0, ab, gb, S):
    T = zr.shape[0]
    W, PW = RWKV_WIDTH, RWKV_PAD_WIDTH
    assert RWKV_DECAY_LORA + RWKV_A_LORA == LANES
    glw = -(-RWKV_GATE_LORA // LANES) * LANES
    assert RWKV_TAIL + LANES + glw <= PW
    tm = _pick(S, 256)
    mu = jnp.zeros((1, PW), F32).at[0, :RWKV_IN_WIDTH].set(rwkv_mu)
    wbp = jnp.zeros((LANES, W), F32).at[:RWKV_DECAY_LORA].set(wb)
    abp = jnp.zeros((LANES, W), F32).at[RWKV_DECAY_LORA:].set(ab)
    gbp = jnp.zeros((glw, W), F32).at[:RWKV_GATE_LORA].set(gb)
    row_spec = pl.BlockSpec((1, W), lambda i: (0, 0))
    out_spec = pl.BlockSpec((tm, W), lambda i: (i, 0))
    return pl.pallas_call(
        functools.partial(_rwkv_prep_kernel, tm=tm, S=S),
        grid=(T // tm,),
        in_specs=[pl.BlockSpec((tm, PW), lambda i: (i, 0)),
                  pl.BlockSpec((SUBLANES, PW), lambda i: (jnp.maximum(i * (tm // SUBLANES) - 1, 0), 0)),
                  pl.BlockSpec((1, PW), lambda i: (0, 0)),
                  row_spec, row_spec,
                  pl.BlockSpec((LANES, W), lambda i: (0, 0)),
                  pl.BlockSpec((LANES, W), lambda i: (0, 0)),
                  pl.BlockSpec((glw, W), lambda i: (0, 0))],
        out_specs=[out_spec] * 6,
        out_shape=[jax.ShapeDtypeStruct((T, W), F32)] * 6,
        compiler_params=_params("parallel"),
        name="rwkv_prep",
    )(zr, zr, mu, w0.reshape(1, W), a0.reshape(1, W), wbp, abp, gbp)


def _rwkv_kernel(r_ref, k_ref, v_ref, ld_ref, a_ref, g_ref, kks_ref, ka_ref, rk_ref, lnw_ref, lnb_ref,
                 o_ref, ht_sc, *, pairs):
    C = RWKV_CHUNK
    N = RWKV_HEAD_DIM
    P2 = 2 * C

    @pl.when(pl.program_id(2) == 0)
    def _():
        ht_sc[...] = jnp.zeros(ht_sc.shape, F32)

    row = lax.broadcasted_iota(I32, (P2, LANES), 0)
    col = lax.broadcasted_iota(I32, (P2, LANES), 1)
    same = (row // C) == (col // N)
    sameq = (row // C) == (col // C)
    incl = sameq & (col <= row)
    strict = sameq & (col < row)
    grp = jnp.where((row // N) == (col // N), 1.0, 0.0).astype(BF16)
    tr = lax.broadcasted_iota(I32, (C, C), 0)
    tc = lax.broadcasted_iota(I32, (C, C), 1)
    lower = jnp.where(tc <= tr, 1.0, 0.0).astype(BF16)

    def split(x):
        hi = x.astype(BF16)
        return hi, (x - hi.astype(F32)).astype(BF16)

    def group_sum(x):
        hi, lo = split(x)
        return _dot(hi, grp) + _dot(lo, grp)

    def stack(x):
        return jnp.where(same, jnp.concatenate([x, x], axis=0), 0.0).astype(BF16)

    P = range(pairs)
    sl = [slice(p * LANES, (p + 1) * LANES) for p in P]

    def each(fn):
        return [fn(p) for p in P]

    r = each(lambda p: r_ref[:, sl[p]])
    kraw = each(lambda p: k_ref[:, sl[p]])
    v = each(lambda p: v_ref[:, sl[p]])
    ld = each(lambda p: ld_ref[:, sl[p]])
    a = each(lambda p: a_ref[:, sl[p]])
    kk = each(lambda p: kraw[p] * kks_ref[:, sl[p]])
    nrm = each(lambda p: group_sum(kk[p] * kk[p]))
    kk = each(lambda p: kk[p] / jnp.maximum(jnp.sqrt(nrm[p]), 1e-12))
    k = each(lambda p: kraw[p] * (1.0 + (a[p] - 1.0) * ka_ref[:, sl[p]]))
    ldp = each(lambda p: split(ld[p]))
    cum = each(lambda p: _dot(lower, ldp[p][0]) + _dot(lower, ldp[p][1]))
    tot = each(lambda p: cum[p][C - 1:C])
    beta = each(lambda p: kk[p] * a[p])
    a_s = each(lambda p: stack(-kk[p] * jnp.exp(cum[p] - ld[p])))
    r_s = each(lambda p: stack(r[p] * jnp.exp(cum[p])))
    e_inv = each(lambda p: jnp.exp(-cum[p]))
    e_rem = each(lambda p: jnp.exp(tot[p] - cum[p]))
    bt_s = each(lambda p: stack(beta[p] * e_inv[p]))
    kt_s = each(lambda p: stack(k[p] * e_inv[p]))
    bh_s = each(lambda p: stack(beta[p] * e_rem[p]))
    kh_s = each(lambda p: stack(k[p] * e_rem[p]))
    v_s = each(lambda p: stack(v[p]))
    quad = each(lambda p: _dot_nt(jnp.concatenate([a_s[p], r_s[p]], axis=0),
                                  jnp.concatenate([bt_s[p], kt_s[p]], axis=0)))
    lp = each(lambda p: jnp.where(strict, quad[p][:P2, :P2], 0.0))
    l_ak = each(lambda p: jnp.where(strict, quad[p][:P2, P2:], 0.0).astype(BF16))
    m_rb = each(lambda p: jnp.where(incl, quad[p][P2:, :P2], 0.0).astype(BF16))
    m_rk = each(lambda p: jnp.where(incl, quad[p][P2:, P2:], 0.0).astype(BF16))
    x = each(lambda p: jnp.concatenate([a_s[p].astype(F32), _dot(l_ak[p], v_s[p])], axis=1))
    steps = int(np.log2(C))
    for s in range(steps):
        lpb = each(lambda p: lp[p].astype(BF16))
        x = each(lambda p: x[p] + _dot(lpb[p], x[p].astype(BF16)))
        if s + 1 < steps:
            lp = each(lambda p: _dot(lpb[p], lpb[p]))
    xb = each(lambda p: x[p].astype(BF16))
    ht = each(lambda p: ht_sc[p])
    htb = each(lambda p: ht[p].astype(BF16))
    qo = each(lambda p: _dot(m_rb[p], xb[p]))
    q_eff = each(lambda p: (r_s[p].astype(F32) + qo[p][:, :LANES]).astype(BF16))
    o = each(lambda p: _dot_nt(q_eff[p], htb[p]) + qo[p][:, LANES:] + _dot(m_rk[p], v_s[p]))
    pgt = each(lambda p: _dot_tn(xb[p], bh_s[p]))
    for p in P:
        ht_sc[p] = (ht[p] * jnp.exp(tot[p]) + _dot(htb[p], pgt[p][:LANES].astype(BF16)) + pgt[p][LANES:]
                    + _dot_tn(v_s[p], kh_s[p]))
    o = each(lambda p: o[p][:C] + o[p][C:])
    mean = each(lambda p: group_sum(o[p]) * (1.0 / N))
    d = each(lambda p: o[p] - mean[p])
    var = each(lambda p: group_sum(d[p] * d[p]) * (1.0 / N))
    bonus = each(lambda p: group_sum(r[p] * k[p] * rk_ref[:, sl[p]]))
    for p in P:
        y = d[p] * lax.rsqrt(var[p] + RWKV_LN_EPS) * lnw_ref[:, sl[p]] + lnb_ref[:, sl[p]]
        o_ref[:, sl[p]] = ((y + bonus[p] * v[p]) * g_ref[:, sl[p]]).astype(o_ref.dtype)


def rwkv_mix(r, k, v, ld, a, g, kk_scale, ka, rk, ln_w, ln_b, B, S):
    T, W = r.shape
    C = RWKV_CHUNK
    assert 2 * RWKV_HEAD_DIM == LANES and 2 * C == LANES and S % C == 0
    pairs = 8
    bw = pairs * LANES
    nc = S // C
    seq = pl.BlockSpec((C, bw), lambda b, p, c: (b * nc + c, p))
    par = pl.BlockSpec((1, bw), lambda b, p, c: (0, p))
    return pl.pallas_call(
        functools.partial(_rwkv_kernel, pairs=pairs),
        grid=(B, W // bw, nc),
        in_specs=[seq] * 6 + [par] * 5,
        out_specs=seq,
        out_shape=jax.ShapeDtypeStruct((T, W), BF16),
        scratch_shapes=[pltpu.VMEM((pairs, LANES, LANES), F32)],
        compiler_params=_params("parallel", "parallel", "arbitrary"),
        name="rwkv_mix",
    )(r, k, v, ld, a, g, *(z.reshape(1, W) for z in (kk_scale, ka, rk, ln_w, ln_b)))


def _merge_kernel(ya_ref, yb_ref, yc_ref, pa_ref, pb_ref, pc_ref, ga_ref, gb_ref, gc_ref, o_ref):
    m = ga_ref[...].astype(F32) * _dot(ya_ref[...], pa_ref[...])
    m = m + gb_ref[...].astype(F32) * _dot(yb_ref[...], pb_ref[...])
    m = m + gc_ref[...].astype(F32) * _dot(yc_ref[...], pc_ref[...])
    o_ref[...] = m.astype(o_ref.dtype)


def branch_merge(ya, yb, yc, pa, pb, pc, gm, D):
    T = ya.shape[0]
    tm, tn = _pick(T, 1024), _pick(D, 512)
    nj = D // tn
    ysp = lambda y: pl.BlockSpec((tm, y.shape[1]), lambda i, j: (i, 0))
    psp = lambda p: pl.BlockSpec((p.shape[0], tn), lambda i, j: (0, j))
    gsp = lambda n: pl.BlockSpec((tm, tn), lambda i, j: (i, n * nj + j))
    return pl.pallas_call(
        _merge_kernel,
        grid=(T // tm, nj),
        in_specs=[ysp(ya), ysp(yb), ysp(yc), psp(pa), psp(pb), psp(pc), gsp(0), gsp(1), gsp(2)],
        out_specs=pl.BlockSpec((tm, tn), lambda i, j: (i, j)),
        out_shape=jax.ShapeDtypeStruct((T, D), BF16),
        compiler_params=_params("parallel", "arbitrary"),
        name="branch_merge",
    )(ya, yb, yc, pa, pb, pc, gm, gm, gm)


MOE_DMA_UNROLL = 32
SCATTER_DMA_PRIORITY = 1


def _moe_kernel(blk_e_ref, asg_ref, h_hbm, wg_ref, bg_ref, wu_ref, bu_ref, wd_ref, bd_ref, y_hbm,
                xbuf, ybuf, gsem, ssem, *, R, n_blk, A):
    i = pl.program_id(0)
    slot = i % 2
    D = xbuf.shape[-1]

    def per_row(fn, unroll=MOE_DMA_UNROLL):
        def body(r, c):
            fn(r)
            return c
        lax.fori_loop(0, R, body, 0, unroll=unroll)

    def start_gather(step, s, unroll=MOE_DMA_UNROLL):
        def one(r):
            tok = jnp.minimum(asg_ref[step * R + r] >> TOP_K_SHIFT, A // TOP_K - 1)
            pltpu.make_async_copy(h_hbm.at[pl.ds(tok, 1)], xbuf.at[s, pl.ds(r, 1)], gsem.at[s]).start()
        per_row(one, unroll)

    def start_scatter(step, s):
        def one(r):
            dst = asg_ref[step * R + r]
            col = pl.multiple_of((dst & (TOP_K - 1)) * D, LANES)
            pltpu.make_async_copy(ybuf.at[s, pl.ds(r, 1)],
                                  y_hbm.at[pl.ds(dst >> TOP_K_SHIFT, 1), pl.ds(col, D)],
                                  ssem.at[s]).start(priority=SCATTER_DMA_PRIORITY)
        per_row(one)

    def wait_gather(s):
        pltpu.make_async_copy(h_hbm.at[pl.ds(0, R)], xbuf.at[s], gsem.at[s]).wait()

    def wait_scatter(s):
        pltpu.make_async_copy(ybuf.at[s], y_hbm.at[pl.ds(0, R), pl.ds(0, D)], ssem.at[s]).wait()

    @pl.when(i == 0)
    def _():
        start_gather(0, 0)

    wait_gather(slot)
    x = jnp.concatenate([part.astype(BF16) for part in _unpack_bf16_pairs(xbuf[slot])], axis=1)
    start_gather(jnp.minimum(i + 1, n_blk - 1), 1 - slot, unroll=True)
    gt = jnp.minimum(_dot(x, wg_ref[...]) + bg_ref[...], SWIGLU_LIMIT)
    up = jnp.clip(_dot(x, wu_ref[...]) + bu_ref[...], -SWIGLU_LIMIT, SWIGLU_LIMIT)
    act = gt * jax.nn.sigmoid(SWIGLU_ALPHA * gt) * (up + 1.0)
    y = _dot(act.astype(BF16), wd_ref[...]) + bd_ref[...]

    @pl.when(i >= 2)
    def _():
        wait_scatter(slot)

    ybuf[slot] = _pack_bf16_pairs(y)
    start_scatter(i, slot)

    @pl.when(i == n_blk - 1)
    def _():
        wait_gather(1 - slot)
        wait_scatter(1 - slot)
        wait_scatter(slot)


def moe_experts(h, blk_e, row_asg, wg, bg, wu, bu, wd, bd, R):
    T = h.shape[0]
    D = wg.shape[1]
    Dw = h.shape[1]
    assert 2 * Dw == D
    E, _, F = wg.shape
    n_blk = blk_e.shape[0]
    A = T * TOP_K
    assert n_blk >= 2 and (n_blk * R) % TOP_K == 0 and 1 << TOP_K_SHIFT == TOP_K
    wspec = lambda shape: pl.BlockSpec((None,) + shape, lambda i, be, ra: (be[i], 0, 0))
    return pl.pallas_call(
        functools.partial(_moe_kernel, R=R, n_blk=n_blk, A=A),
        grid_spec=pltpu.PrefetchScalarGridSpec(
            num_scalar_prefetch=2,
            grid=(n_blk,),
            in_specs=[pl.BlockSpec(memory_space=pl.ANY),
                      wspec((D, F)), wspec((1, F)), wspec((D, F)), wspec((1, F)),
                      wspec((F, D)), wspec((1, D))],
            out_specs=pl.BlockSpec(memory_space=pl.ANY),
            scratch_shapes=[pltpu.VMEM((2, R, Dw), jnp.uint32), pltpu.VMEM((2, R, Dw), jnp.uint32),
                            pltpu.SemaphoreType.DMA((2,)), pltpu.SemaphoreType.DMA((2,))]),
        out_shape=jax.ShapeDtypeStruct((n_blk * R // TOP_K, TOP_K * Dw), jnp.uint32),
        compiler_params=_params("arbitrary"),
        name="moe_experts",
    )(blk_e, row_asg, h, wg, bg.reshape(E, 1, F), wu, bu.reshape(E, 1, F), wd, bd.reshape(E, 1, D))


def _combined(x_ref, y_ref, wt_ref, gate_ref):
    half = x_ref.shape[1] // 2
    wt = wt_ref[...]
    acc_lo = acc_hi = None
    for k in range(TOP_K):
        lo, hi = _unpack_bf16_pairs(y_ref[:, k * half:(k + 1) * half])
        wk = wt[:, k:k + 1]
        acc_lo = wk * lo if acc_lo is None else acc_lo + wk * lo
        acc_hi = wk * hi if acc_hi is None else acc_hi + wk * hi
    return x_ref[...] + gate_ref[0] * jnp.concatenate([acc_lo, acc_hi], axis=1)


def _combine_next_kernel(x_ref, y_ref, wt_ref, gate_ref, g_ref, sc_ref, sh_ref, o_ref, h_ref):
    x = _combined(x_ref, y_ref, wt_ref, gate_ref)
    o_ref[...] = x
    h_ref[...] = _norm_mod(x, g_ref[...], sc_ref[0], sh_ref[0]).astype(h_ref.dtype)


def _combine_final_kernel(x_ref, y_ref, wt_ref, gate_ref, g_ref, o_ref):
    x = _combined(x_ref, y_ref, wt_ref, gate_ref)
    ms = jnp.mean(x * x, axis=-1, keepdims=True)
    o_ref[...] = x * lax.rsqrt(ms + RMS_EPS) * g_ref[...]


def moe_combine(x, y, wt, gate, S, norm_g, scale=None, shift=None):
    T, D = x.shape
    B = gate.shape[0]
    tm = _pick(S, 128)
    spt = S // tm
    row = pl.BlockSpec((tm, D), lambda i: (i, 0))
    per_batch = pl.BlockSpec((1, 1, D), lambda i: (i // spt, 0, 0))
    in_specs = [row, pl.BlockSpec((tm, TOP_K * D // 2), lambda i: (i, 0)),
                pl.BlockSpec((tm, LANES), lambda i: (i, 0)), per_batch,
                pl.BlockSpec((1, D), lambda i: (0, 0))]
    args = [x, y, wt, gate.reshape(B, 1, D), norm_g.reshape(1, D)]
    if scale is None:
        return pl.pallas_call(
            _combine_final_kernel, grid=(T // tm,), in_specs=in_specs, out_specs=row,
            out_shape=jax.ShapeDtypeStruct((T, D), F32),
            compiler_params=_params("parallel"), name="moe_combine_final")(*args)
    return pl.pallas_call(
        _combine_next_kernel, grid=(T // tm,), in_specs=in_specs + [per_batch, per_batch],
        out_specs=[row, row],
        out_shape=[jax.ShapeDtypeStruct((T, D), F32), jax.ShapeDtypeStruct((T, D), BF16)],
        compiler_params=_params("parallel"), name="moe_combine_next",
    )(*args, scale.reshape(B, 1, D), shift.reshape(B, 1, D))


def route_tables(top_idx, rank, counts, E, R):
    T = top_idx.shape[0]
    A = T * TOP_K
    flat_e = top_idx.reshape(A)
    padded = (counts + R - 1) // R * R
    pends = jnp.cumsum(padded)
    onehot = flat_e[:, None] == jnp.arange(E, dtype=I32)[None, :]
    dest = jnp.sum(jnp.where(onehot, (pends - padded)[None, :], 0), axis=1) + rank.reshape(A)
    n_blk = -(-(A + E * (R - 1)) // R)
    blk_start = jnp.arange(n_blk, dtype=I32)[:, None] * R
    blk_e = jnp.minimum(jnp.sum((pends[None, :] <= blk_start).astype(I32), axis=1), E - 1)
    pads_before = jnp.cumsum(padded - counts) - (padded - counts)
    r = blk_start + jnp.arange(R, dtype=I32)[None, :] - (pends - padded)[blk_e][:, None]
    over = r - counts[blk_e][:, None]
    pad_id = (A + pads_before[blk_e][:, None] + over).reshape(n_blk * R)
    real = jnp.full((n_blk * R,), -1, I32).at[dest].set(jnp.arange(A, dtype=I32))
    row_asg = jnp.where(real < 0, pad_id, real)
    return blk_e, row_asg


def _mixer_weights(w_in, D):
    CW, QW, GW = CONV_WIDTH, NSA_Q_WIDTH, NSA_G_WIDTH
    o_q = 3 * CW
    o_kv = o_q + QW
    o_g = o_kv + 6 * GW
    o_r = o_g + NSA_GATE_WIDTH
    o_m = o_r + RWKV_IN_WIDTH
    w_qkv = w_in[:, :o_g].astype(BF16)
    per_group = NSA_GATE_WIDTH // NSA_KV_GROUPS
    gate_blocks = [jnp.pad(w_in[:, o_g + g * per_group:o_g + (g + 1) * per_group],
                           ((0, 0), (0, LANES - per_group))) for g in range(NSA_KV_GROUPS)]
    w_rg = jnp.concatenate(
        [jnp.pad(w_in[:, o_r:o_m], ((0, 0), (0, RWKV_PAD_WIDTH - RWKV_IN_WIDTH)))] + gate_blocks,
        axis=1).astype(BF16)
    w_m = w_in[:, o_m:].astype(BF16)
    return w_qkv, w_rg, w_m


def _layer(x, h, mod, next_norm, final_g, B, S, tables, norm_g, w_in, conv_w, cmp_pe, cmp_w1, cmp_w2, rwkv_mu, rwkv_w0,
           rwkv_wb, rwkv_a0, rwkv_ab, rwkv_gb, rwkv_kk, rwkv_ka, rwkv_rk, rwkv_ln_w, rwkv_ln_b,
           w_branch, w_out, router_w, router_b, exp_wg, exp_bg, exp_wu, exp_bu, exp_wd, exp_bd):
    T, D = x.shape
    CW, QW, GW, Dh, G = CONV_WIDTH, NSA_Q_WIDTH, NSA_G_WIDTH, NSA_HEAD_DIM, NSA_KV_GROUPS
    shift1, scale1, gate1, shift2, scale2, gate2 = (mod[:, n] for n in range(N_MOD))

    w_qkv, w_rg, w_m = _mixer_weights(w_in, D)
    z = mm_qkv(h, w_qkv, tables, S)
    zrg = _mm_call(_mm_plain_kernel, h, w_rg, F32, name="mm_rwkv_gate")
    gm = _mm_call(_mm_sigmoid_kernel, h, w_m, BF16, name="mm_merge_gate")

    y_a = gated_conv(z, conv_w, S)

    ns = S // CMP_STRIDE
    kv0 = 3 * CW + QW
    to_strides = lambda zz: zz.reshape(B, ns, CMP_STRIDE, G, Dh).transpose(0, 3, 1, 2, 4).reshape(
        B, G, ns, CMP_STRIDE * Dh)
    cmp_in = jnp.stack([to_strides(z[:, kv0:kv0 + GW]), to_strides(z[:, kv0 + GW:kv0 + 2 * GW])])
    kv_cmp = compress_kv(cmp_in, cmp_pe, cmp_w1, cmp_w2)
    y_b = nsa_attention(z, zrg, kv_cmp, B, S, q_col=3 * CW, ks_col=kv0 + 2 * GW, vs_col=kv0 + 3 * GW,
                        kw_col=kv0 + 4 * GW, vw_col=kv0 + 5 * GW, gate_col=RWKV_PAD_WIDTH)

    r, k, v, ld, a, g = rwkv_prep(zrg, rwkv_mu, rwkv_w0, rwkv_wb, rwkv_a0, rwkv_ab, rwkv_gb, S)
    y_c = rwkv_mix(r, k, v, ld, a, g, rwkv_kk, rwkv_ka, rwkv_rk.reshape(-1), rwkv_ln_w, rwkv_ln_b, B, S)

    wb16 = w_branch.astype(BF16)
    m = branch_merge(y_a, y_b, y_c, wb16[:CW], wb16[CW:CW + QW], wb16[CW + QW:], gm, D)
    x = mm_residual(m, w_out.astype(BF16), x, gate1, S)

    E = router_w.shape[1]
    h2, top_idx, top_w, counts = norm_router(x, norm_g[1], scale2, shift2, router_w, router_b, S)
    R = _pick(T * TOP_K // E, 256)
    blk_e, row_asg = route_tables(top_idx[:, :TOP_K], top_idx[:, TOP_K:2 * TOP_K], counts[0, :E], E, R)
    y = moe_experts(h2, blk_e, row_asg, exp_wg.astype(BF16), exp_bg, exp_wu.astype(BF16), exp_bu,
                    exp_wd.astype(BF16), exp_bd, R)
    if next_norm is None:
        return moe_combine(x, y, top_w, gate2, S, final_g)
    return moe_combine(x, y, top_w, gate2, S, *next_norm)


def kernel(x, c, ada_w, ada_b, ada_table, norm_g, final_g, w_in, conv_w, cmp_pe, cmp_w1, cmp_w2, rwkv_mu, rwkv_w0, rwkv_wb, rwkv_a0, rwkv_ab, rwkv_gb, rwkv_kk, rwkv_ka, rwkv_rk, rwkv_ln_w, rwkv_ln_b, w_branch, w_out, router_w, router_b, exp_wg, exp_bg, exp_wu, exp_bu, exp_wd, exp_bd):
    B, S, D = x.shape
    depth = w_in.shape[0]
    tables = rope_lane_tables(S)
    mod_all = ada_modulation(c, ada_w, ada_b).reshape(B, N_MOD, D)
    xf = x.reshape(B * S, D)
    mods = [mod_all + ada_table[l] for l in range(depth)]
    h = norm_modulate(xf, norm_g[0, 0], mods[0][:, 1], mods[0][:, 0], S, BF16)
    for l in range(depth):
        next_norm = None if l + 1 == depth else (norm_g[l + 1, 0], mods[l + 1][:, 1], mods[l + 1][:, 0])
        out = _layer(xf, h, mods[l], next_norm, final_g, B, S, tables, norm_g[l], w_in[l], conv_w[l],
                     cmp_pe[l], cmp_w1[l], cmp_w2[l], rwkv_mu[l], rwkv_w0[l], rwkv_wb[l], rwkv_a0[l],
                     rwkv_ab[l], rwkv_gb[l], rwkv_kk[l], rwkv_ka[l], rwkv_rk[l], rwkv_ln_w[l],
                     rwkv_ln_b[l], w_branch[l], w_out[l], router_w[l], router_b[l], exp_wg[l], exp_bg[l],
                     exp_wu[l], exp_bu[l], exp_wd[l], exp_bd[l])
        if next_norm is None:
            return out.reshape(B, S, D)
        xf, h = out
```
